```python
import math
import jax, jax.numpy as jnp
from jax import lax
import numpy as np

D_MODEL = 1024
BATCH = 8
SEQ = 2048
DEPTH = 4
DEC_BATCH = 128
DEC_SEQ = 8
PAST_LEN = 16384
PAGE_SIZE = 128

MIX_WIDTH = D_MODEL
POOL_WIDTH = MIX_WIDTH // 2
SSM_WIDTH = MIX_WIDTH - POOL_WIDTH
POOL_WINDOWS = (2, 4, 8, 16)
N_POOL_GROUPS = len(POOL_WINDOWS)
POOL_GROUP_DIM = POOL_WIDTH // N_POOL_GROUPS
POOL_BUF = max(POOL_WINDOWS) - 1
SSM_GROUP_DIM = 16
N_SSM_GROUPS = SSM_WIDTH // SSM_GROUP_DIM
SSM_STATE = 64
D_FF = -(-8 * D_MODEL // (3 * 256)) * 256
PLE_DIM = 256
EPS = 1e-6
DT_MIN = 0.001
DT_MAX = 0.1

kernel_name = 'hymba_pool_s5_decoder_step'


def rmsnorm(x, g):
    xf = x.astype(jnp.float32)
    y = xf * lax.rsqrt(jnp.mean(xf * xf, axis=-1, keepdims=True) + EPS)
    return (y * g.astype(jnp.float32)).astype(x.dtype)


def pool_mix(u_ext, pos0, w_pool, pool_scale):
    bsz, n_ext, _ = u_ext.shape
    t_len = n_ext - POOL_BUF
    uf = u_ext.astype(jnp.float32).reshape(bsz, n_ext, N_POOL_GROUPS, POOL_GROUP_DIM)
    cs = jnp.concatenate([jnp.zeros_like(uf[:, :1]), jnp.cumsum(uf, axis=1)], axis=1)
    end = cs[:, POOL_BUF + 1:]
    pos = pos0 + jnp.arange(t_len, dtype=jnp.int32)
    means = []
    for g, w in enumerate(POOL_WINDOWS):
        start = cs[:, POOL_BUF + 1 - w: POOL_BUF + 1 - w + t_len, g]
        cnt = jnp.minimum(pos + 1, w).astype(jnp.float32)[None, :, None]
        means.append((end[:, :, g] - start) / cnt)
    diff = (jnp.stack(means, axis=2) - uf[:, POOL_BUF:]).astype(u_ext.dtype)
    y = jnp.einsum('btgc,gcd->btgd', diff, w_pool).reshape(bsz, t_len, POOL_WIDTH)
    return y * pool_scale


def _ssm_combine(e1, e2):
    a1r, a1i, b1r, b1i = e1
    a2r, a2i, b2r, b2i = e2
    return (a2r * a1r - a2i * a1i,
            a2r * a1i + a2i * a1r,
            a2r * b1r - a2i * b1i + b2r,
            a2r * b1i + a2i * b1r + b2i)


def ssm_mix(u, h0_re, h0_im, a_re, a_im, log_dt, b_re, b_im, c_re, c_im, d_skip, w_glu, b_glu):
    f32 = jnp.float32
    bsz, t_len, _ = u.shape
    ug = u.astype(f32).reshape(bsz, t_len, N_SSM_GROUPS, SSM_GROUP_DIM)
    ar, ai = a_re.astype(f32), a_im.astype(f32)
    dt = jnp.exp(log_dt.astype(f32))[:, None]
    mag = jnp.exp(dt * ar)
    lam_re, lam_im = mag * jnp.cos(dt * ai), mag * jnp.sin(dt * ai)
    den = ar * ar + ai * ai
    nr = lam_re - 1.0
    k_re = (nr * ar + lam_im * ai) / den
    k_im = (lam_im * ar - nr * ai) / den
    br, bi = b_re.astype(f32), b_im.astype(f32)
    bb_re = k_re[..., None] * br - k_im[..., None] * bi
    bb_im = k_re[..., None] * bi + k_im[..., None] * br
    bu_re = jnp.einsum('gpc,btgc->btgp', bb_re, ug)
    bu_im = jnp.einsum('gpc,btgc->btgp', bb_im, ug)
    hr0, hi0 = h0_re.astype(f32), h0_im.astype(f32)
    bu_re = bu_re.at[:, 0].add(lam_re * hr0 - lam_im * hi0)
    bu_im = bu_im.at[:, 0].add(lam_re * hi0 + lam_im * hr0)
    la_re = jnp.broadcast_to(lam_re, bu_re.shape)
    la_im = jnp.broadcast_to(lam_im, bu_im.shape)
    _, _, h_re, h_im = lax.associative_scan(_ssm_combine, (la_re, la_im, bu_re, bu_im), axis=1)
    y = (jnp.einsum('gcp,btgp->btgc', c_re.astype(f32), h_re)
         - jnp.einsum('gcp,btgp->btgc', c_im.astype(f32), h_im)
         + d_skip.astype(f32).reshape(N_SSM_GROUPS, SSM_GROUP_DIM) * ug)
    g = jax.nn.gelu(y.reshape(bsz, t_len, SSM_WIDTH).astype(u.dtype), approximate=False)
    out = g * jax.nn.sigmoid(g @ w_glu + b_glu)
    return out, h_re[:, -1].astype(h0_re.dtype), h_im[:, -1].astype(h0_im.dtype)


def layer(x, p, pool_buf, h_re, h_im, pos0, lw):
    (g_mix, w_in, w_pool, pool_scale, a_re, a_im, log_dt, b_re, b_im, c_re, c_im,
     d_skip, w_glu, b_glu, w_out, g_ffn, w_gate_up, w_down, g_ple, w_ple, w_ple_gate) = lw
    z = rmsnorm(x, g_mix) @ w_in
    u_pool, u_ssm = z[..., :POOL_WIDTH], z[..., POOL_WIDTH:]
    u_ext = jnp.concatenate([pool_buf.astype(z.dtype), u_pool], axis=1)
    y_pool = pool_mix(u_ext, pos0, w_pool, pool_scale)
    y_ssm, h_re_new, h_im_new = ssm_mix(u_ssm, h_re, h_im, a_re, a_im, log_dt, b_re, b_im,
                                        c_re, c_im, d_skip, w_glu, b_glu)
    x = x + jnp.concatenate([y_pool, y_ssm], axis=-1) @ w_out
    gu = rmsnorm(x, g_ffn) @ w_gate_up
    x = x + (jax.nn.silu(gu[..., :D_FF]) * gu[..., D_FF:]) @ w_down
    x = x + (p @ w_ple) * jax.nn.sigmoid(rmsnorm(x, g_ple) @ w_ple_gate)
    return x, u_ext[:, -POOL_BUF:], h_re_new, h_im_new


def trunk(x, p, pool_bufs, h_res, h_ims, pos0, weights):
    new_pool, new_re, new_im = [], [], []
    for i in range(DEPTH):
        lw = tuple(w[i] for w in weights)
        x, pb, hr, hi = layer(x, p[i], pool_bufs[i], h_res[i], h_ims[i], pos0, lw)
        new_pool.append(pb)
        new_re.append(hr)
        new_im.append(hi)
    return x, jnp.stack(new_pool), jnp.stack(new_re), jnp.stack(new_im)


def setup_inputs(seed: int = 0) -> dict:
    key = jax.random.key(seed)
    ks = jax.random.split(key, 32)
    nrm = jax.random.normal
    f32 = jnp.float32
    n_idx = jnp.arange(SSM_STATE, dtype=f32)
    a_re = -0.5 + 0.01 * nrm(ks[10], (DEPTH, N_SSM_GROUPS, SSM_STATE), f32)
    a_im = math.pi * n_idx[None, None, :] + 0.01 * nrm(ks[11], (DEPTH, N_SSM_GROUPS, SSM_STATE), f32)
    log_dt = jax.random.uniform(ks[12], (DEPTH, N_SSM_GROUPS), f32, math.log(DT_MIN), math.log(DT_MAX))
    return {
        'x_prompt': nrm(ks[0], (BATCH, SEQ, D_MODEL), f32),
        'x_sample': nrm(ks[1], (DEC_BATCH, DEC_SEQ, D_MODEL), f32),
        'state_pool': nrm(ks[2], (DEPTH, DEC_BATCH, POOL_BUF, POOL_WIDTH), f32),
        'state_ssm_re': 0.5 * nrm(ks[3], (DEPTH, DEC_BATCH, N_SSM_GROUPS, SSM_STATE), f32),
        'state_ssm_im': 0.5 * nrm(ks[4], (DEPTH, DEC_BATCH, N_SSM_GROUPS, SSM_STATE), f32),
        'p_prompt': nrm(ks[5], (DEPTH, BATCH, SEQ, PLE_DIM), f32),
        'p_sample': nrm(ks[6], (DEPTH, DEC_BATCH, DEC_SEQ, PLE_DIM), f32),
        'g_mix': 1.0 + 0.02 * nrm(ks[7], (DEPTH, D_MODEL), f32),
        'w_in': nrm(ks[8], (DEPTH, D_MODEL, MIX_WIDTH), f32) * D_MODEL ** -0.5,
        'w_pool': nrm(ks[9], (DEPTH, N_POOL_GROUPS, POOL_GROUP_DIM, POOL_GROUP_DIM), f32) * POOL_GROUP_DIM ** -0.5,
        'pool_scale': 0.5 + 0.05 * nrm(ks[13], (DEPTH, POOL_WIDTH), f32),
        'ssm_a_re': a_re,
        'ssm_a_im': a_im,
        'ssm_log_dt': log_dt,
        'ssm_b_re': nrm(ks[14], (DEPTH, N_SSM_GROUPS, SSM_STATE, SSM_GROUP_DIM), f32) * (2 * SSM_GROUP_DIM) ** -0.5,
        'ssm_b_im': nrm(ks[15], (DEPTH, N_SSM_GROUPS, SSM_STATE, SSM_GROUP_DIM), f32) * (2 * SSM_GROUP_DIM) ** -0.5,
        'ssm_c_re': nrm(ks[16], (DEPTH, N_SSM_GROUPS, SSM_GROUP_DIM, SSM_STATE), f32) * (2 * SSM_STATE) ** -0.5,
        'ssm_c_im': nrm(ks[17], (DEPTH, N_SSM_GROUPS, SSM_GROUP_DIM, SSM_STATE), f32) * (2 * SSM_STATE) ** -0.5,
        'ssm_d': 0.5 * nrm(ks[18], (DEPTH, SSM_WIDTH), f32),
        'w_glu': nrm(ks[19], (DEPTH, SSM_WIDTH, SSM_WIDTH), f32) * SSM_WIDTH ** -0.5,
        'b_glu': 0.01 * nrm(ks[20], (DEPTH, SSM_WIDTH), f32),
        'w_out': nrm(ks[21], (DEPTH, MIX_WIDTH, D_MODEL), f32) * MIX_WIDTH ** -0.5,
        'g_ffn': 1.0 + 0.02 * nrm(ks[22], (DEPTH, D_MODEL), f32),
        'w_gate_up': nrm(ks[23], (DEPTH, D_MODEL, 2 * D_FF), f32) * D_MODEL ** -0.5,
        'w_down': nrm(ks[24], (DEPTH, D_FF, D_MODEL), f32) * D_FF ** -0.5,
        'g_ple': 1.0 + 0.02 * nrm(ks[25], (DEPTH, D_MODEL), f32),
        'w_ple': nrm(ks[26], (DEPTH, PLE_DIM, D_MODEL), f32) * PLE_DIM ** -0.5,
        'w_ple_gate': nrm(ks[27], (DEPTH, D_MODEL, D_MODEL), f32) * D_MODEL ** -0.5,
        'g_final': 1.0 + 0.02 * nrm(ks[28], (D_MODEL,), f32),
    }


def reference(x_prompt, x_sample, state_pool, state_ssm_re, state_ssm_im, p_prompt, p_sample,
              g_mix, w_in, w_pool, pool_scale, ssm_a_re, ssm_a_im, ssm_log_dt, ssm_b_re, ssm_b_im,
              ssm_c_re, ssm_c_im, ssm_d, w_glu, b_glu, w_out, g_ffn, w_gate_up, w_down,
              g_ple, w_ple, w_ple_gate, g_final):
    weights = (g_mix, w_in, w_pool, pool_scale, ssm_a_re, ssm_a_im, ssm_log_dt, ssm_b_re, ssm_b_im,
               ssm_c_re, ssm_c_im, ssm_d, w_glu, b_glu, w_out, g_ffn, w_gate_up, w_down,
               g_ple, w_ple, w_ple_gate)
    bsz = x_prompt.shape[0]
    pool0 = jnp.zeros((DEPTH, bsz, POOL_BUF, POOL_WIDTH), x_prompt.dtype)
    h0 = jnp.zeros((DEPTH, bsz, N_SSM_GROUPS, SSM_STATE), state_ssm_re.dtype)
    xp, pool_p, re_p, im_p = trunk(x_prompt, p_prompt, pool0, h0, h0, 0, weights)
    xs, pool_s, re_s, im_s = trunk(x_sample, p_sample, state_pool, state_ssm_re, state_ssm_im,
                                   PAST_LEN, weights)
    y_prompt = rmsnorm(xp, g_final)
    y_sample = rmsnorm(xs, g_final)
    return (y_prompt, y_sample, pool_p, re_p, im_p, pool_s, re_s, im_s)
```

```python
import functools

import numpy as np
import jax
import jax.numpy as jnp
from jax import lax
from jax.experimental import pallas as pl
from jax.experimental.pallas import tpu as pltpu

LANE = 128
SUBLANE = 8
VMEM_LIMIT_BYTES = 60 * 1024 * 1024

POOL_WINDOWS = (2, 4, 8, 16)
POOL_BUF = max(POOL_WINDOWS) - 1
SSM_GROUP_DIM = 16
SSM_STATE = 64
EPS = 1e-6
PAST_LEN = 16384

SSM_BLOCK_CH = LANE
SSM_BLOCK_GROUPS = SSM_BLOCK_CH // SSM_GROUP_DIM
SSM_BLOCK_STATES = SSM_BLOCK_GROUPS * SSM_STATE

FFN_CHUNK = 256

_SQRT_HALF = np.float32(np.sqrt(0.5))


def _rms(x, g):
    y = x * lax.rsqrt(jnp.mean(x * x, axis=-1, keepdims=True) + EPS)
    return y * g


def _bdot(a, b):
    return jnp.dot(a.astype(jnp.bfloat16), b, preferred_element_type=jnp.float32)


def _ssm_prep_kernel(a_re_ref, a_im_ref, log_dt_ref, b_re_ref, b_im_ref,
                     lam_re_ref, lam_im_ref, bb_re_ref, bb_im_ref):
    ar = a_re_ref[...]
    ai = a_im_ref[...]
    dt = jnp.exp(log_dt_ref[...])
    mag = jnp.exp(dt * ar)
    lam_re = mag * jnp.cos(dt * ai)
    lam_im = mag * jnp.sin(dt * ai)
    den = ar * ar + ai * ai
    nr = lam_re - 1.0
    k_re = (nr * ar + lam_im * ai) / den
    k_im = (lam_im * ar - nr * ai) / den
    lam_re_ref[...] = lam_re
    lam_im_ref[...] = lam_im
    for c in range(SSM_GROUP_DIM):
        br = b_re_ref[c]
        bi = b_im_ref[c]
        bb_re_ref[c] = k_re * br - k_im * bi
        bb_im_ref[c] = k_re * bi + k_im * br


def _ssm_prep(a_re, a_im, log_dt, b_re, b_im):
    depth, n_groups, n_state = a_re.shape
    b_re_t = jnp.transpose(b_re, (0, 3, 1, 2))
    b_im_t = jnp.transpose(b_im, (0, 3, 1, 2))
    gp = pl.BlockSpec((None, n_groups, n_state), lambda l: (l, 0, 0))
    cgp = pl.BlockSpec((None, SSM_GROUP_DIM, n_groups, n_state), lambda l: (l, 0, 0, 0))
    return pl.pallas_call(
        _ssm_prep_kernel,
        grid=(depth,),
        in_specs=[gp, gp, pl.BlockSpec((None, n_groups, 1), lambda l: (l, 0, 0)), cgp, cgp],
        out_specs=[gp, gp, cgp, cgp],
        out_shape=[jax.ShapeDtypeStruct(a_re.shape, jnp.float32)] * 2
        + [jax.ShapeDtypeStruct(b_re_t.shape, jnp.float32)] * 2,
        name="ssm_prep",
    )(a_re, a_im, log_dt[..., None], b_re_t, b_im_t)


def _block_diag(w):
    depth, nb, gb, r, c = w.shape
    eye = jnp.eye(gb, dtype=w.dtype)
    out = w[:, :, :, :, None, :] * eye[None, None, :, None, :, None]
    return out.reshape(depth, nb, gb * r, gb * c)


def _layer_kernel(*refs, tc, bb, carry, pos0, final_norm, n_ssm_blocks, d_ff):
    it = iter(refs)
    x_ref = next(it)
    p_ref = next(it)
    if not carry:
        hist_in_ref = next(it)
        h_re_in_ref = next(it)
        h_im_in_ref = next(it)
    (g_mix_ref, w_in_ref, w_pool_ref, pool_scale_ref, lam_re_ref, lam_im_ref,
     b_ref, c_ref, d_skip_ref, w_glu_ref, b_glu_ref, w_out_ref, g_ffn_ref,
     w_gu_ref, w_down_ref, g_ple_ref, w_ple_ref, w_ple_gate_ref, g_final_ref) = (
         next(it) for _ in range(19))
    o_ref = next(it)
    hist_out_ref = next(it)
    h_re_ref = next(it)
    h_im_ref = next(it)
    ue_ref = next(it)
    bu_ref = next(it)
    mix_ref = next(it)
    act_ref = next(it)

    m = tc * bb
    hb = POOL_BUF * bb
    pool_width = ue_ref.shape[1]
    pid = pl.program_id(0)

    if carry:
        @pl.when(pid == 0)
        def _():
            ue_ref[0:hb, :] = jnp.zeros((hb, pool_width), jnp.float32)
            h_re_ref[...] = jnp.zeros(h_re_ref.shape, jnp.float32)
            h_im_ref[...] = jnp.zeros(h_im_ref.shape, jnp.float32)
    else:
        ue_ref[0:hb, :] = hist_in_ref[...]
        h_re_ref[...] = h_re_in_ref[...]
        h_im_ref[...] = h_im_in_ref[...]

    x = x_ref[...]
    z = _bdot(_rms(x, g_mix_ref[...]), w_in_ref[...])
    ue_ref[hb:hb + m, :] = z[:, :pool_width]
    u_ssm = z[:, pool_width:]

    t_local = lax.shift_right_logical(
        lax.broadcasted_iota(jnp.int32, (m, LANE), 0), int(np.log2(bb)))
    pos = t_local + (pos0 + (pid * tc if carry else 0))
    for g, w in enumerate(POOL_WINDOWS):
        cols = slice(g * LANE, (g + 1) * LANE)
        cur = ue_ref[hb:hb + m, cols]
        s = cur
        for j in range(1, w):
            s = s + ue_ref[hb - j * bb:hb - j * bb + m, cols]
        cnt = jnp.minimum(pos + 1, w).astype(jnp.float32)
        y = _bdot(s / cnt - cur, w_pool_ref[g]) * pool_scale_ref[:, cols]
        mix_ref[:, cols] = y.astype(jnp.bfloat16)
    hist_out_ref[...] = ue_ref[m:m + hb, :]
    if carry:
        ue_ref[0:hb, :] = ue_ref[m:m + hb, :]

    bs = SSM_BLOCK_STATES
    for i in range(n_ssm_blocks):
        u_i = u_ssm[:, i * SSM_BLOCK_CH:(i + 1) * SSM_BLOCK_CH]
        bu_ref[:, 2 * bs * i:2 * bs * (i + 1)] = _bdot(u_i, b_ref[i])
    for i in range(n_ssm_blocks):
        c_re, c_im, s0 = 2 * bs * i, 2 * bs * i + bs, bs * i
        lre = jnp.broadcast_to(lam_re_ref[:, s0:s0 + bs], (SUBLANE, bs))
        lim = jnp.broadcast_to(lam_im_ref[:, s0:s0 + bs], (SUBLANE, bs))
        for rg in range(bb // SUBLANE):
            r_off = rg * SUBLANE

            def step(t, h, c_re=c_re, c_im=c_im, r_off=r_off, lre=lre, lim=lim):
                h_re, h_im = h
                rows = pl.ds(pl.multiple_of(t * bb + r_off, SUBLANE), SUBLANE)
                n_re = lre * h_re - lim * h_im + bu_ref[rows, c_re:c_re + bs]
                n_im = lre * h_im + lim * h_re + bu_ref[rows, c_im:c_im + bs]
                bu_ref[rows, c_re:c_re + bs] = n_re
                bu_ref[rows, c_im:c_im + bs] = n_im
                return n_re, n_im

            h0 = (h_re_ref[r_off:r_off + SUBLANE, s0:s0 + bs],
                  h_im_ref[r_off:r_off + SUBLANE, s0:s0 + bs])
            h_re, h_im = lax.fori_loop(0, tc, step, h0, unroll=min(tc, 8))
            h_re_ref[r_off:r_off + SUBLANE, s0:s0 + bs] = h_re
            h_im_ref[r_off:r_off + SUBLANE, s0:s0 + bs] = h_im
    ys = []
    for i in range(n_ssm_blocks):
        cols = slice(i * SSM_BLOCK_CH, (i + 1) * SSM_BLOCK_CH)
        ys.append(_bdot(bu_ref[:, 2 * bs * i:2 * bs * (i + 1)], c_ref[i])
                  + d_skip_ref[:, cols] * u_ssm[:, cols])
    y = jnp.concatenate(ys, axis=-1)
    gl = 0.5 * y * (1.0 + lax.erf(y * _SQRT_HALF))
    y_ssm = gl * jax.nn.sigmoid(_bdot(gl, w_glu_ref[...]) + b_glu_ref[...])
    mix_ref[:, pool_width:] = y_ssm.astype(jnp.bfloat16)

    x = x + jnp.dot(mix_ref[...], w_out_ref[...], preferred_element_type=jnp.float32)

    hn = _rms(x, g_ffn_ref[...]).astype(jnp.bfloat16)
    for j in range(d_ff // FFN_CHUNK):
        c0 = j * FFN_CHUNK
        gate = jnp.dot(hn, w_gu_ref[:, c0:c0 + FFN_CHUNK], preferred_element_type=jnp.float32)
        up = jnp.dot(hn, w_gu_ref[:, d_ff + c0:d_ff + c0 + FFN_CHUNK],
                     preferred_element_type=jnp.float32)
        act_ref[:, c0:c0 + FFN_CHUNK] = (gate * jax.nn.sigmoid(gate) * up).astype(jnp.bfloat16)
    x = x + jnp.dot(act_ref[...], w_down_ref[...], preferred_element_type=jnp.float32)

    gate = jax.nn.sigmoid(_bdot(_rms(x, g_ple_ref[...]), w_ple_gate_ref[...]))
    x = x + _bdot(p_ref[...], w_ple_ref[...]) * gate
    if final_norm:
        x = _rms(x, g_final_ref[...])
    o_ref[...] = x


def _layer_call(layer, x, p, state, weights, g_final, *, tc, bb, carry, pos0, final_norm, name):
    rows, d_model = x.shape
    m = tc * bb
    n_tiles = rows // m
    hb = POOL_BUF * bb
    (g_mix, w_in, w_pool, pool_scale, lam_re, lam_im, b_cat, c_cat, d_skip, w_glu, b_glu,
     w_out, g_ffn, w_gu, w_down, g_ple, w_ple, w_ple_gate) = weights
    pool_width = pool_scale.shape[-1]
    n_ssm_blocks = b_cat.shape[1]
    n_state_cols = lam_re.shape[-1]
    d_ff = w_down.shape[1]
    ple_dim = p.shape[-1]

    def tile(width):
        return pl.BlockSpec((m, width), lambda i: (i, 0))

    def resident(arr):
        nd = arr.ndim - 1
        return pl.BlockSpec((None,) + arr.shape[1:], lambda i, nd=nd: (layer,) + (0,) * nd,
                            pipeline_mode=pl.Buffered(1))

    in_specs = [tile(d_model),
                pl.BlockSpec((None, m, ple_dim), lambda i: (layer, i, 0))]
    args = [x, p]
    if carry:
        state_spec = lambda r, c: pl.BlockSpec((r, c), lambda i: (0, 0))
    else:
        state_spec = lambda r, c: pl.BlockSpec((r, c), lambda i: (i, 0))
        hist, h_re, h_im = state
        in_specs += [state_spec(hb, pool_width), state_spec(bb, n_state_cols),
                     state_spec(bb, n_state_cols)]
        args += [hist, h_re, h_im]
    in_specs += [resident(w) for w in weights]
    in_specs.append(pl.BlockSpec(g_final.shape, lambda i: (0, 0), pipeline_mode=pl.Buffered(1)))
    args += list(weights) + [g_final]

    n_state_rows = bb if carry else n_tiles * bb
    n_hist_rows = hb if carry else n_tiles * hb
    out_shape = [jax.ShapeDtypeStruct((rows, d_model), jnp.float32),
                 jax.ShapeDtypeStruct((n_hist_rows, pool_width), jnp.float32),
                 jax.ShapeDtypeStruct((n_state_rows, n_state_cols), jnp.float32),
                 jax.ShapeDtypeStruct((n_state_rows, n_state_cols), jnp.float32)]
    out_specs = [tile(d_model), state_spec(hb, pool_width), state_spec(bb, n_state_cols),
                 state_spec(bb, n_state_cols)]
    scratch = [pltpu.VMEM((hb + m, pool_width), jnp.float32),
               pltpu.VMEM((m, 2 * n_state_cols), jnp.float32),
               pltpu.VMEM((m, w_out.shape[1]), jnp.bfloat16),
               pltpu.VMEM((m, d_ff), jnp.bfloat16)]
    kern = functools.partial(_layer_kernel, tc=tc, bb=bb, carry=carry, pos0=pos0,
                             final_norm=final_norm, n_ssm_blocks=n_ssm_blocks, d_ff=d_ff)
    return pl.pallas_call(
        kern,
        grid=(n_tiles,),
        in_specs=in_specs,
        out_specs=out_specs,
        out_shape=out_shape,
        scratch_shapes=scratch,
        compiler_params=pltpu.CompilerParams(dimension_semantics=("arbitrary",),
                                             vmem_limit_bytes=VMEM_LIMIT_BYTES),
        name=name,
    )(*args)


PROMPT_TC = 32
SAMPLE_BB = 32


def kernel(x_prompt, x_sample, state_pool, state_ssm_re, state_ssm_im, p_prompt, p_sample, g_mix, w_in, w_pool, pool_scale, ssm_a_re, ssm_a_im, ssm_log_dt, ssm_b_re, ssm_b_im, ssm_c_re, ssm_c_im, ssm_d, w_glu, b_glu, w_out, g_ffn, w_gate_up, w_down, g_ple, w_ple, w_ple_gate, g_final):
    depth = w_in.shape[0]
    batch, seq, d_model = x_prompt.shape
    dec_batch, dec_seq, _ = x_sample.shape
    n_groups, n_state = ssm_a_re.shape[1:]
    n_state_cols = n_groups * n_state
    pool_width = pool_scale.shape[-1]
    ple_dim = p_prompt.shape[-1]
    n_ssm_blocks = n_groups // SSM_BLOCK_GROUPS
    bf16 = jnp.bfloat16

    lam_re, lam_im, bb_re, bb_im = _ssm_prep(ssm_a_re, ssm_a_im, ssm_log_dt, ssm_b_re, ssm_b_im)

    def b_blocks(bb):
        w = jnp.transpose(bb, (0, 2, 1, 3)).reshape(
            depth, n_ssm_blocks, SSM_BLOCK_GROUPS, SSM_GROUP_DIM, n_state)
        return _block_diag(w)

    def c_blocks(c):
        w = jnp.transpose(c, (0, 1, 3, 2)).reshape(
            depth, n_ssm_blocks, SSM_BLOCK_GROUPS, n_state, SSM_GROUP_DIM)
        return _block_diag(w)

    b_cat = jnp.concatenate([b_blocks(bb_re), b_blocks(bb_im)], axis=-1).astype(bf16)
    c_cat = jnp.concatenate([c_blocks(ssm_c_re), -c_blocks(ssm_c_im)], axis=-2).astype(bf16)
    row = lambda a: a[:, None, :]
    weights = (row(g_mix), w_in.astype(bf16), w_pool.astype(bf16), row(pool_scale),
               lam_re.reshape(depth, 1, n_state_cols), lam_im.reshape(depth, 1, n_state_cols),
               b_cat, c_cat, row(ssm_d), w_glu.astype(bf16), row(b_glu), w_out.astype(bf16),
               row(g_ffn), w_gate_up.astype(bf16), w_down.astype(bf16), row(g_ple),
               w_ple.astype(bf16), w_ple_gate.astype(bf16))
    g_fin = g_final[None, :]

    xp = jnp.transpose(x_prompt, (1, 0, 2)).reshape(seq * batch, d_model)
    pp = jnp.transpose(p_prompt, (0, 2, 1, 3)).reshape(depth, seq * batch, ple_dim)
    pool_p, re_p, im_p = [], [], []
    for l in range(depth):
        xp, hist, h_re, h_im = _layer_call(
            l, xp, pp, None, weights, g_fin, tc=PROMPT_TC, bb=batch, carry=True, pos0=0,
            final_norm=(l == depth - 1), name=f"prompt_layer{l}")
        pool_p.append(jnp.transpose(hist.reshape(POOL_BUF, batch, pool_width), (1, 0, 2)))
        re_p.append(h_re.reshape(batch, n_groups, n_state))
        im_p.append(h_im.reshape(batch, n_groups, n_state))
    y_prompt = jnp.transpose(xp.reshape(seq, batch, d_model), (1, 0, 2))

    bb = SAMPLE_BB
    nt = dec_batch // bb

    def to_tiles(a):
        t, w = a.shape[1:]
        return jnp.transpose(a.reshape(nt, bb, t, w), (0, 2, 1, 3)).reshape(nt * t * bb, w)

    def from_tiles(a, t):
        w = a.shape[-1]
        return jnp.transpose(a.reshape(nt, t, bb, w), (0, 2, 1, 3)).reshape(dec_batch, t, w)

    xs = to_tiles(x_sample)
    ps = jnp.stack([to_tiles(p_sample[l]) for l in range(depth)])
    pool_s, re_s, im_s = [], [], []
    for l in range(depth):
        state = (to_tiles(state_pool[l]),
                 state_ssm_re[l].reshape(dec_batch, n_state_cols),
                 state_ssm_im[l].reshape(dec_batch, n_state_cols))
        xs, hist, h_re, h_im = _layer_call(
            l, xs, ps, state, weights, g_fin, tc=dec_seq, bb=bb, carry=False, pos0=PAST_LEN,
            final_norm=(l == depth - 1), name=f"sample_layer{l}")
        pool_s.append(from_tiles(hist, POOL_BUF))
        re_s.append(h_re.reshape(dec_batch, n_groups, n_state))
        im_s.append(h_im.reshape(dec_batch, n_groups, n_state))
    y_sample = from_tiles(xs, dec_seq)

    return (y_prompt, y_sample, jnp.stack(pool_p), jnp.stack(re_p), jnp.stack(im_p),
            jnp.stack(pool_s), jnp.stack(re_s), jnp.stack(im_s))
```

```python
import functools

import numpy as np
import jax
import jax.numpy as jnp
from jax import lax
from jax.experimental import pallas as pl
from jax.experimental.pallas import tpu as pltpu

LANE = 128
SUBLANE = 8
VMEM_LIMIT_BYTES = 60 * 1024 * 1024

POOL_WINDOWS = (2, 4, 8, 16)
POOL_BUF = max(POOL_WINDOWS) - 1
SSM_GROUP_DIM = 16
SSM_STATE = 64
EPS = 1e-6
PAST_LEN = 16384

SSM_BLOCK_CH = LANE
SSM_BLOCK_GROUPS = SSM_BLOCK_CH // SSM_GROUP_DIM
SSM_BLOCK_STATES = SSM_BLOCK_GROUPS * SSM_STATE

FFN_CHUNK = 256

_SQRT_HALF = np.float32(np.sqrt(0.5))


def _rms(x, g):
    y = x * lax.rsqrt(jnp.mean(x * x, axis=-1, keepdims=True) + EPS)
    return y * g


def _bdot(a, b):
    return jnp.dot(a.astype(jnp.bfloat16), b, preferred_element_type=jnp.float32)


def _ssm_prep_kernel(a_re_ref, a_im_ref, log_dt_ref, b_re_ref, b_im_ref,
                     lam_re_ref, lam_im_ref, bb_re_ref, bb_im_ref):
    ar = a_re_ref[...]
    ai = a_im_ref[...]
    dt = jnp.exp(log_dt_ref[...])
    mag = jnp.exp(dt * ar)
    lam_re = mag * jnp.cos(dt * ai)
    lam_im = mag * jnp.sin(dt * ai)
    den = ar * ar + ai * ai
    nr = lam_re - 1.0
    k_re = (nr * ar + lam_im * ai) / den
    k_im = (lam_im * ar - nr * ai) / den
    lam_re_ref[...] = lam_re
    lam_im_ref[...] = lam_im
    for c in range(SSM_GROUP_DIM):
        br = b_re_ref[c]
        bi = b_im_ref[c]
        bb_re_ref[c] = k_re * br - k_im * bi
        bb_im_ref[c] = k_re * bi + k_im * br


def _ssm_prep(a_re, a_im, log_dt, b_re, b_im):
    depth, n_groups, n_state = a_re.shape
    b_re_t = jnp.transpose(b_re, (0, 3, 1, 2))
    b_im_t = jnp.transpose(b_im, (0, 3, 1, 2))
    gp = pl.BlockSpec((None, n_groups, n_state), lambda l: (l, 0, 0))
    cgp = pl.BlockSpec((None, SSM_GROUP_DIM, n_groups, n_state), lambda l: (l, 0, 0, 0))
    return pl.pallas_call(
        _ssm_prep_kernel,
        grid=(depth,),
        in_specs=[gp, gp, pl.BlockSpec((None, n_groups, 1), lambda l: (l, 0, 0)), cgp, cgp],
        out_specs=[gp, gp, cgp, cgp],
        out_shape=[jax.ShapeDtypeStruct(a_re.shape, jnp.float32)] * 2
        + [jax.ShapeDtypeStruct(b_re_t.shape, jnp.float32)] * 2,
        name="ssm_prep",
    )(a_re, a_im, log_dt[..., None], b_re_t, b_im_t)


def _block_diag(w):
    depth, nb, gb, r, c = w.shape
    eye = jnp.eye(gb, dtype=w.dtype)
    out = w[:, :, :, :, None, :] * eye[None, None, :, None, :, None]
    return out.reshape(depth, nb, gb * r, gb * c)


def _layer_kernel(*refs, tc, bb, carry, pos0, final_norm, n_ssm_blocks, d_ff):
    it = iter(refs)
    x_ref = next(it)
    p_ref = next(it)
    if not carry:
        hist_in_ref = next(it)
        h_re_in_ref = next(it)
        h_im_in_ref = next(it)
    (g_mix_ref, w_in_ref, w_pool_ref, pool_scale_ref, lam_re_ref, lam_im_ref,
     b_ref, c_ref, d_skip_ref, w_glu_ref, b_glu_ref, w_out_ref, g_ffn_ref,
     w_gu_ref, w_down_ref, g_ple_ref, w_ple_ref, w_ple_gate_ref, g_final_ref) = (
         next(it) for _ in range(19))
    o_ref = next(it)
    hist_out_ref = next(it)
    h_re_ref = next(it)
    h_im_ref = next(it)
    ue_ref = next(it)
    bu_ref = next(it)
    mix_ref = next(it)
    act_ref = next(it)
    lam_b_ref = next(it)

    m = tc * bb
    hb = POOL_BUF * bb
    pool_width = ue_ref.shape[1]
    pid = pl.program_id(0)

    if carry:
        @pl.when(pid == 0)
        def _():
            ue_ref[0:hb, :] = jnp.zeros((hb, pool_width), jnp.float32)
            h_re_ref[...] = jnp.zeros(h_re_ref.shape, jnp.float32)
            h_im_ref[...] = jnp.zeros(h_im_ref.shape, jnp.float32)
    else:
        ue_ref[0:hb, :] = hist_in_ref[...]
        h_re_ref[...] = h_re_in_ref[...]
        h_im_ref[...] = h_im_in_ref[...]

    x = x_ref[...]
    z = _bdot(_rms(x, g_mix_ref[...]), w_in_ref[...])
    ue_ref[hb:hb + m, :] = z[:, :pool_width]
    u_ssm = z[:, pool_width:]

    t_local = lax.shift_right_logical(
        lax.broadcasted_iota(jnp.int32, (m, LANE), 0), int(np.log2(bb)))
    pos = t_local + (pos0 + (pid * tc if carry else 0))
    for g, w in enumerate(POOL_WINDOWS):
        cols = slice(g * LANE, (g + 1) * LANE)
        cur = ue_ref[hb:hb + m, cols]
        s = cur
        for j in range(1, w):
            s = s + ue_ref[hb - j * bb:hb - j * bb + m, cols]
        cnt = jnp.minimum(pos + 1, w).astype(jnp.float32)
        y = _bdot(s / cnt - cur, w_pool_ref[g]) * pool_scale_ref[:, cols]
        mix_ref[:, cols] = y.astype(jnp.bfloat16)
    hist_out_ref[...] = ue_ref[m:m + hb, :]
    if carry:
        ue_ref[0:hb, :] = ue_ref[m:m + hb, :]

    bs = SSM_BLOCK_STATES
    for i in range(n_ssm_blocks):
        u_i = u_ssm[:, i * SSM_BLOCK_CH:(i + 1) * SSM_BLOCK_CH]
        bu_ref[:, 2 * bs * i:2 * bs * (i + 1)] = _bdot(u_i, b_ref[i])
    n_cols = n_ssm_blocks * bs
    lam_b_ref[0] = jnp.broadcast_to(lam_re_ref[...], (SUBLANE, n_cols))
    lam_b_ref[1] = jnp.broadcast_to(lam_im_ref[...], (SUBLANE, n_cols))
    for rg in range(bb // SUBLANE):
        r_off = rg * SUBLANE

        def step(t, h, r_off=r_off):
            rows = pl.ds(pl.multiple_of(t * bb + r_off, SUBLANE), SUBLANE)
            new = []
            for i in range(n_ssm_blocks):
                c_re, c_im, s0 = 2 * bs * i, 2 * bs * i + bs, bs * i
                h_re, h_im = h[2 * i], h[2 * i + 1]
                lre = lam_b_ref[0, :, s0:s0 + bs]
                lim = lam_b_ref[1, :, s0:s0 + bs]
                n_re = lre * h_re - lim * h_im + bu_ref[rows, c_re:c_re + bs]
                n_im = lre * h_im + lim * h_re + bu_ref[rows, c_im:c_im + bs]
                bu_ref[rows, c_re:c_re + bs] = n_re
                bu_ref[rows, c_im:c_im + bs] = n_im
                new += [n_re, n_im]
            return tuple(new)

        h0 = []
        for i in range(n_ssm_blocks):
            h0 += [h_re_ref[r_off:r_off + SUBLANE, bs * i:bs * (i + 1)],
                   h_im_ref[r_off:r_off + SUBLANE, bs * i:bs * (i + 1)]]
        h = lax.fori_loop(0, tc, step, tuple(h0), unroll=min(tc, 8))
        for i in range(n_ssm_blocks):
            h_re_ref[r_off:r_off + SUBLANE, bs * i:bs * (i + 1)] = h[2 * i]
            h_im_ref[r_off:r_off + SUBLANE, bs * i:bs * (i + 1)] = h[2 * i + 1]
    ys = []
    for i in range(n_ssm_blocks):
        cols = slice(i * SSM_BLOCK_CH, (i + 1) * SSM_BLOCK_CH)
        ys.append(_bdot(bu_ref[:, 2 * bs * i:2 * bs * (i + 1)], c_ref[i])
                  + d_skip_ref[:, cols] * u_ssm[:, cols])
    y = jnp.concatenate(ys, axis=-1)
    gl = 0.5 * y * (1.0 + lax.erf(y * _SQRT_HALF))
    y_ssm = gl * jax.nn.sigmoid(_bdot(gl, w_glu_ref[...]) + b_glu_ref[...])
    mix_ref[:, pool_width:] = y_ssm.astype(jnp.bfloat16)

    x = x + jnp.dot(mix_ref[...], w_out_ref[...], preferred_element_type=jnp.float32)

    hn = _rms(x, g_ffn_ref[...]).astype(jnp.bfloat16)
    for j in range(d_ff // FFN_CHUNK):
        c0 = j * FFN_CHUNK
        gate = jnp.dot(hn, w_gu_ref[:, c0:c0 + FFN_CHUNK], preferred_element_type=jnp.float32)
        up = jnp.dot(hn, w_gu_ref[:, d_ff + c0:d_ff + c0 + FFN_CHUNK],
                     preferred_element_type=jnp.float32)
        act_ref[:, c0:c0 + FFN_CHUNK] = (gate * jax.nn.sigmoid(gate) * up).astype(jnp.bfloat16)
    x = x + jnp.dot(act_ref[...], w_down_ref[...], preferred_element_type=jnp.float32)

    gate = jax.nn.sigmoid(_bdot(_rms(x, g_ple_ref[...]), w_ple_gate_ref[...]))
    x = x + _bdot(p_ref[...], w_ple_ref[...]) * gate
    if final_norm:
        x = _rms(x, g_final_ref[...])
    o_ref[...] = x


def _layer_call(layer, x, p, state, weights, g_final, *, tc, bb, carry, pos0, final_norm, name):
    rows, d_model = x.shape
    m = tc * bb
    n_tiles = rows // m
    hb = POOL_BUF * bb
    (g_mix, w_in, w_pool, pool_scale, lam_re, lam_im, b_cat, c_cat, d_skip, w_glu, b_glu,
     w_out, g_ffn, w_gu, w_down, g_ple, w_ple, w_ple_gate) = weights
    pool_width = pool_scale.shape[-1]
    n_ssm_blocks = b_cat.shape[1]
    n_state_cols = lam_re.shape[-1]
    d_ff = w_down.shape[1]
    ple_dim = p.shape[-1]

    def tile(width):
        return pl.BlockSpec((m, width), lambda i: (i, 0))

    def resident(arr):
        nd = arr.ndim - 1
        return pl.BlockSpec((None,) + arr.shape[1:], lambda i, nd=nd: (layer,) + (0,) * nd,
                            pipeline_mode=pl.Buffered(1))

    in_specs = [tile(d_model),
                pl.BlockSpec((None, m, ple_dim), lambda i: (layer, i, 0))]
    args = [x, p]
    if carry:
        state_spec = lambda r, c: pl.BlockSpec((r, c), lambda i: (0, 0))
    else:
        state_spec = lambda r, c: pl.BlockSpec((r, c), lambda i: (i, 0))
        hist, h_re, h_im = state
        in_specs += [state_spec(hb, pool_width), state_spec(bb, n_state_cols),
                     state_spec(bb, n_state_cols)]
        args += [hist, h_re, h_im]
    in_specs += [resident(w) for w in weights]
    in_specs.append(pl.BlockSpec(g_final.shape, lambda i: (0, 0), pipeline_mode=pl.Buffered(1)))
    args += list(weights) + [g_final]

    n_state_rows = bb if carry else n_tiles * bb
    n_hist_rows = hb if carry else n_tiles * hb
    out_shape = [jax.ShapeDtypeStruct((rows, d_model), jnp.float32),
                 jax.ShapeDtypeStruct((n_hist_rows, pool_width), jnp.float32),
                 jax.ShapeDtypeStruct((n_state_rows, n_state_cols), jnp.float32),
                 jax.ShapeDtypeStruct((n_state_rows, n_state_cols), jnp.float32)]
    out_specs = [tile(d_model), state_spec(hb, pool_width), state_spec(bb, n_state_cols),
                 state_spec(bb, n_state_cols)]
    scratch = [pltpu.VMEM((hb + m, pool_width), jnp.float32),
               pltpu.VMEM((m, 2 * n_state_cols), jnp.float32),
               pltpu.VMEM((m, w_out.shape[1]), jnp.bfloat16),
               pltpu.VMEM((m, d_ff), jnp.bfloat16),
               pltpu.VMEM((2, SUBLANE, n_state_cols), jnp.float32)]
    kern = functools.partial(_layer_kernel, tc=tc, bb=bb, carry=carry, pos0=pos0,
                             final_norm=final_norm, n_ssm_blocks=n_ssm_blocks, d_ff=d_ff)
    return pl.pallas_call(
        kern,
        grid=(n_tiles,),
        in_specs=in_specs,
        out_specs=out_specs,
        out_shape=out_shape,
        scratch_shapes=scratch,
        compiler_params=pltpu.CompilerParams(dimension_semantics=("arbitrary",),
                                             vmem_limit_bytes=VMEM_LIMIT_BYTES),
        name=name,
    )(*args)


PROMPT_TC = 64
SAMPLE_BB = 64


def kernel(x_prompt, x_sample, state_pool, state_ssm_re, state_ssm_im, p_prompt, p_sample, g_mix, w_in, w_pool, pool_scale, ssm_a_re, ssm_a_im, ssm_log_dt, ssm_b_re, ssm_b_im, ssm_c_re, ssm_c_im, ssm_d, w_glu, b_glu, w_out, g_ffn, w_gate_up, w_down, g_ple, w_ple, w_ple_gate, g_final):
    depth = w_in.shape[0]
    batch, seq, d_model = x_prompt.shape
    dec_batch, dec_seq, _ = x_sample.shape
    n_groups, n_state = ssm_a_re.shape[1:]
    n_state_cols = n_groups * n_state
    pool_width = pool_scale.shape[-1]
    ple_dim = p_prompt.shape[-1]
    n_ssm_blocks = n_groups // SSM_BLOCK_GROUPS
    bf16 = jnp.bfloat16

    lam_re, lam_im, bb_re, bb_im = _ssm_prep(ssm_a_re, ssm_a_im, ssm_log_dt, ssm_b_re, ssm_b_im)

    def b_blocks(bb):
        w = jnp.transpose(bb, (0, 2, 1, 3)).reshape(
            depth, n_ssm_blocks, SSM_BLOCK_GROUPS, SSM_GROUP_DIM, n_state)
        return _block_diag(w)

    def c_blocks(c):
        w = jnp.transpose(c, (0, 1, 3, 2)).reshape(
            depth, n_ssm_blocks, SSM_BLOCK_GROUPS, n_state, SSM_GROUP_DIM)
        return _block_diag(w)

    b_cat = jnp.concatenate([b_blocks(bb_re), b_blocks(bb_im)], axis=-1).astype(bf16)
    c_cat = jnp.concatenate([c_blocks(ssm_c_re), -c_blocks(ssm_c_im)], axis=-2).astype(bf16)
    row = lambda a: a[:, None, :]
    weights = (row(g_mix), w_in.astype(bf16), w_pool.astype(bf16), row(pool_scale),
               lam_re.reshape(depth, 1, n_state_cols), lam_im.reshape(depth, 1, n_state_cols),
               b_cat, c_cat, row(ssm_d), w_glu.astype(bf16), row(b_glu), w_out.astype(bf16),
               row(g_ffn), w_gate_up.astype(bf16), w_down.astype(bf16), row(g_ple),
               w_ple.astype(bf16), w_ple_gate.astype(bf16))
    g_fin = g_final[None, :]

    xp = jnp.transpose(x_prompt, (1, 0, 2)).reshape(seq * batch, d_model)
    pp = jnp.transpose(p_prompt.astype(bf16), (0, 2, 1, 3)).reshape(depth, seq * batch, ple_dim)
    pool_p, re_p, im_p = [], [], []
    for l in range(depth):
        xp, hist, h_re, h_im = _layer_call(
            l, xp, pp, None, weights, g_fin, tc=PROMPT_TC, bb=batch, carry=True, pos0=0,
            final_norm=(l == depth - 1), name=f"prompt_layer{l}")
        pool_p.append(jnp.transpose(hist.reshape(POOL_BUF, batch, pool_width), (1, 0, 2)))
        re_p.append(h_re.reshape(batch, n_groups, n_state))
        im_p.append(h_im.reshape(batch, n_groups, n_state))
    y_prompt = jnp.transpose(xp.reshape(seq, batch, d_model), (1, 0, 2))

    bb = SAMPLE_BB
    nt = dec_batch // bb

    def to_tiles(a):
        t, w = a.shape[1:]
        return jnp.transpose(a.reshape(nt, bb, t, w), (0, 2, 1, 3)).reshape(nt * t * bb, w)

    def from_tiles(a, t):
        w = a.shape[-1]
        return jnp.transpose(a.reshape(nt, t, bb, w), (0, 2, 1, 3)).reshape(dec_batch, t, w)

    xs = to_tiles(x_sample)
    ps = jnp.stack([to_tiles(p_sample[l].astype(bf16)) for l in range(depth)])
    pool_s, re_s, im_s = [], [], []
    for l in range(depth):
        state = (to_tiles(state_pool[l]),
                 state_ssm_re[l].reshape(dec_batch, n_state_cols),
                 state_ssm_im[l].reshape(dec_batch, n_state_cols))
        xs, hist, h_re, h_im = _layer_call(
            l, xs, ps, state, weights, g_fin, tc=dec_seq, bb=bb, carry=False, pos0=PAST_LEN,
            final_norm=(l == depth - 1), name=f"sample_layer{l}")
        pool_s.append(from_tiles(hist, POOL_BUF))
        re_s.append(h_re.reshape(dec_batch, n_groups, n_state))
        im_s.append(h_im.reshape(dec_batch, n_groups, n_state))
    y_sample = from_tiles(xs, dec_seq)

    return (y_prompt, y_sample, jnp.stack(pool_p), jnp.stack(re_p), jnp.stack(im_p),
            jnp.stack(pool_s), jnp.stack(re_s), jnp.stack(im_s))
```

```python
import functools

import numpy as np
import jax
import jax.numpy as jnp
from jax import lax
from jax.experimental import pallas as pl
from jax.experimental.pallas import tpu as pltpu

LANE = 128
SUBLANE = 8
VMEM_LIMIT_BYTES = 60 * 1024 * 1024

POOL_WINDOWS = (2, 4, 8, 16)
POOL_BUF = max(POOL_WINDOWS) - 1
SSM_GROUP_DIM = 16
SSM_STATE = 64
EPS = 1e-6
PAST_LEN = 16384

SSM_BLOCK_CH = LANE
SSM_BLOCK_GROUPS = SSM_BLOCK_CH // SSM_GROUP_DIM
SSM_BLOCK_STATES = SSM_BLOCK_GROUPS * SSM_STATE

FFN_CHUNK = 256

_SQRT_HALF = np.float32(np.sqrt(0.5))


def _rms(x, g):
    y = x * lax.rsqrt(jnp.mean(x * x, axis=-1, keepdims=True) + EPS)
    return y * g


def _bdot(a, b):
    return jnp.dot(a.astype(jnp.bfloat16), b, preferred_element_type=jnp.float32)


def _ssm_prep_kernel(a_re_ref, a_im_ref, log_dt_ref, b_re_ref, b_im_ref,
                     lam_re_ref, lam_im_ref, bb_re_ref, bb_im_ref):
    ar = a_re_ref[...]
    ai = a_im_ref[...]
    dt = jnp.exp(log_dt_ref[...])
    mag = jnp.exp(dt * ar)
    lam_re = mag * jnp.cos(dt * ai)
    lam_im = mag * jnp.sin(dt * ai)
    den = ar * ar + ai * ai
    nr = lam_re - 1.0
    k_re = (nr * ar + lam_im * ai) / den
    k_im = (lam_im * ar - nr * ai) / den
    lam_re_ref[...] = lam_re
    lam_im_ref[...] = lam_im
    for c in range(SSM_GROUP_DIM):
        br = b_re_ref[c]
        bi = b_im_ref[c]
        bb_re_ref[c] = k_re * br - k_im * bi
        bb_im_ref[c] = k_re * bi + k_im * br


def _ssm_prep(a_re, a_im, log_dt, b_re, b_im):
    depth, n_groups, n_state = a_re.shape
    b_re_t = jnp.transpose(b_re, (0, 3, 1, 2))
    b_im_t = jnp.transpose(b_im, (0, 3, 1, 2))
    gp = pl.BlockSpec((None, n_groups, n_state), lambda l: (l, 0, 0))
    cgp = pl.BlockSpec((None, SSM_GROUP_DIM, n_groups, n_state), lambda l: (l, 0, 0, 0))
    return pl.pallas_call(
        _ssm_prep_kernel,
        grid=(depth,),
        in_specs=[gp, gp, pl.BlockSpec((None, n_groups, 1), lambda l: (l, 0, 0)), cgp, cgp],
        out_specs=[gp, gp, cgp, cgp],
        out_shape=[jax.ShapeDtypeStruct(a_re.shape, jnp.float32)] * 2
        + [jax.ShapeDtypeStruct(b_re_t.shape, jnp.float32)] * 2,
        name="ssm_prep",
    )(a_re, a_im, log_dt[..., None], b_re_t, b_im_t)


def _block_diag(w):
    depth, nb, gb, r, c = w.shape
    eye = jnp.eye(gb, dtype=w.dtype)
    out = w[:, :, :, :, None, :] * eye[None, None, :, None, :, None]
    return out.reshape(depth, nb, gb * r, gb * c)


def _layer_kernel(*refs, tc, bb, carry, pos0, final_norm, n_ssm_blocks, d_ff):
    it = iter(refs)
    x_ref = next(it)
    p_ref = next(it)
    if not carry:
        hist_in_ref = next(it)
        h_re_in_ref = next(it)
        h_im_in_ref = next(it)
    (g_mix_ref, w_in_ref, w_pool_ref, pool_scale_ref, lam_re_ref, lam_im_ref,
     b_ref, c_ref, d_skip_ref, w_glu_ref, b_glu_ref, w_out_ref, g_ffn_ref,
     w_gu_ref, w_down_ref, g_ple_ref, w_ple_ref, w_ple_gate_ref, g_final_ref) = (
         next(it) for _ in range(19))
    o_ref = next(it)
    hist_out_ref = next(it)
    h_re_ref = next(it)
    h_im_ref = next(it)
    ue_ref = next(it)
    bu_ref = next(it)
    mix_ref = next(it)
    act_ref = next(it)
    lam_b_ref = next(it)
    if carry:
        xs_ref = next(it)

    m = tc * bb
    hb = POOL_BUF * bb
    pool_width = ue_ref.shape[1]
    pid = pl.program_id(0)

    if carry:
        a_slot = lax.rem(pid, 2)
        b_slot = 1 - a_slot
        a_valid = pid < pl.num_programs(0) - 1

        @pl.when(pid == 0)
        def _():
            ue_ref[0:hb, :] = jnp.zeros((hb, pool_width), jnp.float32)
            h_re_ref[...] = jnp.zeros(h_re_ref.shape, jnp.float32)
            h_im_ref[...] = jnp.zeros(h_im_ref.shape, jnp.float32)
            xs_ref[1] = jnp.zeros(xs_ref.shape[1:], jnp.float32)
            mix_ref[1] = jnp.zeros(mix_ref.shape[1:], jnp.bfloat16)
    else:
        a_slot = b_slot = 0
        ue_ref[0:hb, :] = hist_in_ref[...]
        h_re_ref[...] = h_re_in_ref[...]
        h_im_ref[...] = h_im_in_ref[...]

    x = x_ref[...]
    if carry:
        xs_ref[a_slot] = x
    z = _bdot(_rms(x, g_mix_ref[...]), w_in_ref[...])
    ue_ref[hb:hb + m, :] = z[:, :pool_width]
    u_ssm = z[:, pool_width:]

    t_local = lax.shift_right_logical(
        lax.broadcasted_iota(jnp.int32, (m, LANE), 0), int(np.log2(bb)))
    pos = t_local + (pos0 + (pid * tc if carry else 0))
    for g, w in enumerate(POOL_WINDOWS):
        cols = slice(g * LANE, (g + 1) * LANE)
        cur = ue_ref[hb:hb + m, cols]
        s = cur
        for j in range(1, w):
            s = s + ue_ref[hb - j * bb:hb - j * bb + m, cols]
        cnt = jnp.minimum(pos + 1, w).astype(jnp.float32)
        y = _bdot(s / cnt - cur, w_pool_ref[g]) * pool_scale_ref[:, cols]
        mix_ref[a_slot, :, cols] = y.astype(jnp.bfloat16)
    hist_out_ref[...] = ue_ref[m:m + hb, :]
    if carry:
        ue_ref[0:hb, :] = ue_ref[m:m + hb, :]

    bs = SSM_BLOCK_STATES
    for i in range(n_ssm_blocks):
        u_i = u_ssm[:, i * SSM_BLOCK_CH:(i + 1) * SSM_BLOCK_CH]
        bu_ref[:, 2 * bs * i:2 * bs * (i + 1)] = _bdot(u_i, b_ref[i])
    n_cols = n_ssm_blocks * bs
    lam_b_ref[0] = jnp.broadcast_to(lam_re_ref[...], (SUBLANE, n_cols))
    lam_b_ref[1] = jnp.broadcast_to(lam_im_ref[...], (SUBLANE, n_cols))
    for rg in range(bb // SUBLANE):
        r_off = rg * SUBLANE

        def step(t, h, r_off=r_off):
            rows = pl.ds(pl.multiple_of(t * bb + r_off, SUBLANE), SUBLANE)
            new = []
            for i in range(n_ssm_blocks):
                c_re, c_im, s0 = 2 * bs * i, 2 * bs * i + bs, bs * i
                h_re, h_im = h[2 * i], h[2 * i + 1]
                lre = lam_b_ref[0, :, s0:s0 + bs]
                lim = lam_b_ref[1, :, s0:s0 + bs]
                n_re = lre * h_re - lim * h_im + bu_ref[rows, c_re:c_re + bs]
                n_im = lre * h_im + lim * h_re + bu_ref[rows, c_im:c_im + bs]
                bu_ref[rows, c_re:c_re + bs] = n_re
                bu_ref[rows, c_im:c_im + bs] = n_im
                new += [n_re, n_im]
            return tuple(new)

        h0 = []
        for i in range(n_ssm_blocks):
            h0 += [h_re_ref[r_off:r_off + SUBLANE, bs * i:bs * (i + 1)],
                   h_im_ref[r_off:r_off + SUBLANE, bs * i:bs * (i + 1)]]
        h = lax.fori_loop(0, tc, step, tuple(h0), unroll=True if carry else min(tc, 8))
        if carry:
            h = tuple(jnp.where(a_valid, new, old) for new, old in zip(h, h0))
        for i in range(n_ssm_blocks):
            h_re_ref[r_off:r_off + SUBLANE, bs * i:bs * (i + 1)] = h[2 * i]
            h_im_ref[r_off:r_off + SUBLANE, bs * i:bs * (i + 1)] = h[2 * i + 1]
    ys = []
    for i in range(n_ssm_blocks):
        cols = slice(i * SSM_BLOCK_CH, (i + 1) * SSM_BLOCK_CH)
        ys.append(_bdot(bu_ref[:, 2 * bs * i:2 * bs * (i + 1)], c_ref[i])
                  + d_skip_ref[:, cols] * u_ssm[:, cols])
    y = jnp.concatenate(ys, axis=-1)
    gl = 0.5 * y * (1.0 + lax.erf(y * _SQRT_HALF))
    y_ssm = gl * jax.nn.sigmoid(_bdot(gl, w_glu_ref[...]) + b_glu_ref[...])
    mix_ref[a_slot, :, pool_width:] = y_ssm.astype(jnp.bfloat16)

    if carry:
        x = xs_ref[b_slot]
    x = x + jnp.dot(mix_ref[b_slot], w_out_ref[...], preferred_element_type=jnp.float32)

    hn = _rms(x, g_ffn_ref[...]).astype(jnp.bfloat16)
    for j in range(d_ff // FFN_CHUNK):
        c0 = j * FFN_CHUNK
        gate = jnp.dot(hn, w_gu_ref[:, c0:c0 + FFN_CHUNK], preferred_element_type=jnp.float32)
        up = jnp.dot(hn, w_gu_ref[:, d_ff + c0:d_ff + c0 + FFN_CHUNK],
                     preferred_element_type=jnp.float32)
        act_ref[:, c0:c0 + FFN_CHUNK] = (gate * jax.nn.sigmoid(gate) * up).astype(jnp.bfloat16)
    x = x + jnp.dot(act_ref[...], w_down_ref[...], preferred_element_type=jnp.float32)

    gate = jax.nn.sigmoid(_bdot(_rms(x, g_ple_ref[...]), w_ple_gate_ref[...]))
    x = x + _bdot(p_ref[...], w_ple_ref[...]) * gate
    if final_norm:
        x = _rms(x, g_final_ref[...])
    o_ref[...] = x


def _layer_call(layer, x, p, state, weights, g_final, *, tc, bb, carry, pos0, final_norm, name):
    rows, d_model = x.shape
    m = tc * bb
    n_tiles = rows // m
    hb = POOL_BUF * bb
    (g_mix, w_in, w_pool, pool_scale, lam_re, lam_im, b_cat, c_cat, d_skip, w_glu, b_glu,
     w_out, g_ffn, w_gu, w_down, g_ple, w_ple, w_ple_gate) = weights
    pool_width = pool_scale.shape[-1]
    n_ssm_blocks = b_cat.shape[1]
    n_state_cols = lam_re.shape[-1]
    d_ff = w_down.shape[1]
    ple_dim = p.shape[-1]

    last = n_tiles - 1
    a_tile = (lambda i: jnp.minimum(i, last)) if carry else (lambda i: i)
    b_tile = (lambda i: jnp.maximum(i - 1, 0)) if carry else (lambda i: i)

    def resident(arr):
        nd = arr.ndim - 1
        return pl.BlockSpec((None,) + arr.shape[1:], lambda i, nd=nd: (layer,) + (0,) * nd,
                            pipeline_mode=pl.Buffered(1))

    in_specs = [pl.BlockSpec((m, d_model), lambda i: (a_tile(i), 0)),
                pl.BlockSpec((None, m, ple_dim), lambda i: (layer, b_tile(i), 0))]
    args = [x, p]
    if carry:
        state_spec = lambda r, c: pl.BlockSpec((r, c), lambda i: (0, 0))
    else:
        state_spec = lambda r, c: pl.BlockSpec((r, c), lambda i: (i, 0))
        hist, h_re, h_im = state
        in_specs += [state_spec(hb, pool_width), state_spec(bb, n_state_cols),
                     state_spec(bb, n_state_cols)]
        args += [hist, h_re, h_im]
    in_specs += [resident(w) for w in weights]
    in_specs.append(pl.BlockSpec(g_final.shape, lambda i: (0, 0), pipeline_mode=pl.Buffered(1)))
    args += list(weights) + [g_final]

    n_state_rows = bb if carry else n_tiles * bb
    n_hist_rows = hb if carry else n_tiles * hb
    out_shape = [jax.ShapeDtypeStruct((rows, d_model), jnp.float32),
                 jax.ShapeDtypeStruct((n_hist_rows, pool_width), jnp.float32),
                 jax.ShapeDtypeStruct((n_state_rows, n_state_cols), jnp.float32),
                 jax.ShapeDtypeStruct((n_state_rows, n_state_cols), jnp.float32)]
    out_specs = [pl.BlockSpec((m, d_model), lambda i: (b_tile(i), 0)),
                 state_spec(hb, pool_width), state_spec(bb, n_state_cols),
                 state_spec(bb, n_state_cols)]
    n_slots = 2 if carry else 1
    scratch = [pltpu.VMEM((hb + m, pool_width), jnp.float32),
               pltpu.VMEM((m, 2 * n_state_cols), jnp.float32),
               pltpu.VMEM((n_slots, m, w_out.shape[1]), jnp.bfloat16),
               pltpu.VMEM((m, d_ff), jnp.bfloat16),
               pltpu.VMEM((2, SUBLANE, n_state_cols), jnp.float32)]
    if carry:
        scratch.append(pltpu.VMEM((2, m, d_model), jnp.float32))
    kern = functools.partial(_layer_kernel, tc=tc, bb=bb, carry=carry, pos0=pos0,
                             final_norm=final_norm, n_ssm_blocks=n_ssm_blocks, d_ff=d_ff)
    return pl.pallas_call(
        kern,
        grid=(n_tiles + 1 if carry else n_tiles,),
        in_specs=in_specs,
        out_specs=out_specs,
        out_shape=out_shape,
        scratch_shapes=scratch,
        compiler_params=pltpu.CompilerParams(dimension_semantics=("arbitrary",),
                                             vmem_limit_bytes=VMEM_LIMIT_BYTES),
        name=name,
    )(*args)


PROMPT_TC = 32
SAMPLE_BB = 64


def kernel(x_prompt, x_sample, state_pool, state_ssm_re, state_ssm_im, p_prompt, p_sample, g_mix, w_in, w_pool, pool_scale, ssm_a_re, ssm_a_im, ssm_log_dt, ssm_b_re, ssm_b_im, ssm_c_re, ssm_c_im, ssm_d, w_glu, b_glu, w_out, g_ffn, w_gate_up, w_down, g_ple, w_ple, w_ple_gate, g_final):
    depth = w_in.shape[0]
    batch, seq, d_model = x_prompt.shape
    dec_batch, dec_seq, _ = x_sample.shape
    n_groups, n_state = ssm_a_re.shape[1:]
    n_state_cols = n_groups * n_state
    pool_width = pool_scale.shape[-1]
    ple_dim = p_prompt.shape[-1]
    n_ssm_blocks = n_groups // SSM_BLOCK_GROUPS
    bf16 = jnp.bfloat16

    lam_re, lam_im, bb_re, bb_im = _ssm_prep(ssm_a_re, ssm_a_im, ssm_log_dt, ssm_b_re, ssm_b_im)

    def b_blocks(bb):
        w = jnp.transpose(bb, (0, 2, 1, 3)).reshape(
            depth, n_ssm_blocks, SSM_BLOCK_GROUPS, SSM_GROUP_DIM, n_state)
        return _block_diag(w)

    def c_blocks(c):
        w = jnp.transpose(c, (0, 1, 3, 2)).reshape(
            depth, n_ssm_blocks, SSM_BLOCK_GROUPS, n_state, SSM_GROUP_DIM)
        return _block_diag(w)

    b_cat = jnp.concatenate([b_blocks(bb_re), b_blocks(bb_im)], axis=-1).astype(bf16)
    c_cat = jnp.concatenate([c_blocks(ssm_c_re), -c_blocks(ssm_c_im)], axis=-2).astype(bf16)
    row = lambda a: a[:, None, :]
    weights = (row(g_mix), w_in.astype(bf16), w_pool.astype(bf16), row(pool_scale),
               lam_re.reshape(depth, 1, n_state_cols), lam_im.reshape(depth, 1, n_state_cols),
               b_cat, c_cat, row(ssm_d), w_glu.astype(bf16), row(b_glu), w_out.astype(bf16),
               row(g_ffn), w_gate_up.astype(bf16), w_down.astype(bf16), row(g_ple),
               w_ple.astype(bf16), w_ple_gate.astype(bf16))
    g_fin = g_final[None, :]

    xp = jnp.transpose(x_prompt, (1, 0, 2)).reshape(seq * batch, d_model)
    pp = jnp.transpose(p_prompt.astype(bf16), (0, 2, 1, 3)).reshape(depth, seq * batch, ple_dim)
    pool_p, re_p, im_p = [], [], []
    for l in range(depth):
        xp, hist, h_re, h_im = _layer_call(
            l, xp, pp, None, weights, g_fin, tc=PROMPT_TC, bb=batch, carry=True, pos0=0,
            final_norm=(l == depth - 1), name=f"prompt_layer{l}")
        pool_p.append(jnp.transpose(hist.reshape(POOL_BUF, batch, pool_width), (1, 0, 2)))
        re_p.append(h_re.reshape(batch, n_groups, n_state))
        im_p.append(h_im.reshape(batch, n_groups, n_state))
    y_prompt = jnp.transpose(xp.reshape(seq, batch, d_model), (1, 0, 2))

    bb = SAMPLE_BB
    nt = dec_batch // bb

    def to_tiles(a):
        t, w = a.shape[1:]
        return jnp.transpose(a.reshape(nt, bb, t, w), (0, 2, 1, 3)).reshape(nt * t * bb, w)

    def from_tiles(a, t):
        w = a.shape[-1]
        return jnp.transpose(a.reshape(nt, t, bb, w), (0, 2, 1, 3)).reshape(dec_batch, t, w)

    xs = to_tiles(x_sample)
    ps = jnp.stack([to_tiles(p_sample[l].astype(bf16)) for l in range(depth)])
    pool_s, re_s, im_s = [], [], []
    for l in range(depth):
        state = (to_tiles(state_pool[l]),
                 state_ssm_re[l].reshape(dec_batch, n_state_cols),
                 state_ssm_im[l].reshape(dec_batch, n_state_cols))
        xs, hist, h_re, h_im = _layer_call(
            l, xs, ps, state, weights, g_fin, tc=dec_seq, bb=bb, carry=False, pos0=PAST_LEN,
            final_norm=(l == depth - 1), name=f"sample_layer{l}")
        pool_s.append(from_tiles(hist, POOL_BUF))
        re_s.append(h_re.reshape(dec_batch, n_groups, n_state))
        im_s.append(h_im.reshape(dec_batch, n_groups, n_state))
    y_sample = from_tiles(xs, dec_seq)

    return (y_prompt, y_sample, jnp.stack(pool_p), jnp.stack(re_p), jnp.stack(im_p),
            jnp.stack(pool_s), jnp.stack(re_s), jnp.stack(im_s))
```

```python
import functools

import numpy as np
import jax
import jax.numpy as jnp
from jax import lax
from jax.experimental import pallas as pl
from jax.experimental.pallas import tpu as pltpu

LANE = 128
SUBLANE = 8
VMEM_LIMIT_BYTES = 60 * 1024 * 1024

POOL_WINDOWS = (2, 4, 8, 16)
POOL_BUF = max(POOL_WINDOWS) - 1
SSM_GROUP_DIM = 16
SSM_STATE = 64
EPS = 1e-6
PAST_LEN = 16384

SSM_BLOCK_CH = LANE
SSM_BLOCK_GROUPS = SSM_BLOCK_CH // SSM_GROUP_DIM
SSM_BLOCK_STATES = SSM_BLOCK_GROUPS * SSM_STATE

FFN_CHUNK = 256

_SQRT_HALF = np.float32(np.sqrt(0.5))


def _rms(x, g):
    y = x * lax.rsqrt(jnp.mean(x * x, axis=-1, keepdims=True) + EPS)
    return y * g


def _bdot(a, b):
    return jnp.dot(a.astype(jnp.bfloat16), b, preferred_element_type=jnp.float32)


def _ssm_prep_kernel(a_re_ref, a_im_ref, log_dt_ref, b_re_ref, b_im_ref,
                     lam_re_ref, lam_im_ref, bb_re_ref, bb_im_ref):
    ar = a_re_ref[...]
    ai = a_im_ref[...]
    dt = jnp.exp(log_dt_ref[...])
    mag = jnp.exp(dt * ar)
    lam_re = mag * jnp.cos(dt * ai)
    lam_im = mag * jnp.sin(dt * ai)
    den = ar * ar + ai * ai
    nr = lam_re - 1.0
    k_re = (nr * ar + lam_im * ai) / den
    k_im = (lam_im * ar - nr * ai) / den
    lam_re_ref[...] = lam_re
    lam_im_ref[...] = lam_im
    for c in range(SSM_GROUP_DIM):
        br = b_re_ref[c]
        bi = b_im_ref[c]
        bb_re_ref[c] = k_re * br - k_im * bi
        bb_im_ref[c] = k_re * bi + k_im * br


def _ssm_prep(a_re, a_im, log_dt, b_re, b_im):
    depth, n_groups, n_state = a_re.shape
    b_re_t = jnp.transpose(b_re, (0, 3, 1, 2))
    b_im_t = jnp.transpose(b_im, (0, 3, 1, 2))
    gp = pl.BlockSpec((None, n_groups, n_state), lambda l: (l, 0, 0))
    cgp = pl.BlockSpec((None, SSM_GROUP_DIM, n_groups, n_state), lambda l: (l, 0, 0, 0))
    return pl.pallas_call(
        _ssm_prep_kernel,
        grid=(depth,),
        in_specs=[gp, gp, pl.BlockSpec((None, n_groups, 1), lambda l: (l, 0, 0)), cgp, cgp],
        out_specs=[gp, gp, cgp, cgp],
        out_shape=[jax.ShapeDtypeStruct(a_re.shape, jnp.float32)] * 2
        + [jax.ShapeDtypeStruct(b_re_t.shape, jnp.float32)] * 2,
        name="ssm_prep",
    )(a_re, a_im, log_dt[..., None], b_re_t, b_im_t)


def _block_diag(w):
    depth, nb, gb, r, c = w.shape
    eye = jnp.eye(gb, dtype=w.dtype)
    out = w[:, :, :, :, None, :] * eye[None, None, :, None, :, None]
    return out.reshape(depth, nb, gb * r, gb * c)


def _start_all(copies):
    for c in copies:
        c.start()


def _wait_all(copies):
    for c in copies:
        c.wait()


def _layer_kernel(*refs, layer, tc, bb, carry, pos0, final_norm, n_ssm_blocks, d_ff):
    it = iter(refs)
    x_ref = next(it)
    p_ref = next(it)
    if not carry:
        hist_in_ref = next(it)
        h_re_in_ref = next(it)
        h_im_in_ref = next(it)
    (g_mix_ref, w_in_ref, w_pool_ref, pool_scale_ref, lam_re_ref, lam_im_ref,
     b_ref, c_ref, d_skip_ref, w_glu_ref, b_glu_ref, w_out_ref, g_ffn_ref,
     w_gu_ref, w_down_ref, g_ple_ref, w_ple_ref, w_ple_gate_ref, g_final_ref) = (
         next(it) for _ in range(19))
    o_ref = next(it)
    hist_out_ref = next(it)
    h_re_ref = next(it)
    h_im_ref = next(it)
    ue_ref = next(it)
    bu_ref = next(it)
    mix_ref = next(it)
    act_ref = next(it)
    lam_b_ref = next(it)
    if carry:
        xbuf_ref = next(it)
        pbuf_ref = next(it)
        obuf_ref = next(it)
        in_sem = next(it)
        out_sem = next(it)

    m = tc * bb
    hb = POOL_BUF * bb
    pool_width = ue_ref.shape[1]
    pid = pl.program_id(0)
    n_steps = pl.num_programs(0)

    if carry:
        slot = lax.rem(pid, 2)

        def x_copies(tile, s):
            return [pltpu.make_async_copy(x_ref.at[b, pl.ds(tile * tc, tc), :],
                                          xbuf_ref.at[s, :, b, :], in_sem.at[0, s])
                    for b in range(bb)]

        def p_copies(tile, s):
            return [pltpu.make_async_copy(p_ref.at[layer, b, pl.ds(tile * tc, tc), :],
                                          pbuf_ref.at[s, :, b, :], in_sem.at[1, s])
                    for b in range(bb)]

        def o_copies(tile, s):
            return [pltpu.make_async_copy(obuf_ref.at[s, :, b, :],
                                          o_ref.at[b, pl.ds(tile * tc, tc), :], out_sem.at[s])
                    for b in range(bb)]

        @pl.when(pid == 0)
        def _():
            _start_all(x_copies(0, 0) + p_copies(0, 0))
            ue_ref[0:hb, :] = jnp.zeros((hb, pool_width), jnp.float32)
            h_re_ref[...] = jnp.zeros(h_re_ref.shape, jnp.float32)
            h_im_ref[...] = jnp.zeros(h_im_ref.shape, jnp.float32)

        @pl.when(pid + 1 < n_steps)
        def _():
            _start_all(x_copies(pid + 1, 1 - slot) + p_copies(pid + 1, 1 - slot))

        _wait_all(x_copies(pid, slot) + p_copies(pid, slot))
        x = xbuf_ref[slot].reshape(m, xbuf_ref.shape[-1])
        p = pbuf_ref[slot].reshape(m, pbuf_ref.shape[-1])
    else:
        ue_ref[0:hb, :] = hist_in_ref[...]
        h_re_ref[...] = h_re_in_ref[...]
        h_im_ref[...] = h_im_in_ref[...]
        x = x_ref[...]
        p = p_ref[...]

    z = _bdot(_rms(x, g_mix_ref[...]), w_in_ref[...])
    ue_ref[hb:hb + m, :] = z[:, :pool_width]
    u_ssm = z[:, pool_width:]

    t_local = lax.shift_right_logical(
        lax.broadcasted_iota(jnp.int32, (m, LANE), 0), int(np.log2(bb)))
    pos = t_local + (pos0 + (pid * tc if carry else 0))
    for g, w in enumerate(POOL_WINDOWS):
        cols = slice(g * LANE, (g + 1) * LANE)
        cur = ue_ref[hb:hb + m, cols]
        s = cur
        for j in range(1, w):
            s = s + ue_ref[hb - j * bb:hb - j * bb + m, cols]
        cnt = jnp.minimum(pos + 1, w).astype(jnp.float32)
        y = _bdot(s / cnt - cur, w_pool_ref[g]) * pool_scale_ref[:, cols]
        mix_ref[:, cols] = y.astype(jnp.bfloat16)
    hist_out_ref[...] = ue_ref[m:m + hb, :]
    if carry:
        ue_ref[0:hb, :] = ue_ref[m:m + hb, :]

    bs = SSM_BLOCK_STATES
    for i in range(n_ssm_blocks):
        u_i = u_ssm[:, i * SSM_BLOCK_CH:(i + 1) * SSM_BLOCK_CH]
        bu_ref[:, 2 * bs * i:2 * bs * (i + 1)] = _bdot(u_i, b_ref[i])
    n_cols = n_ssm_blocks * bs
    lam_b_ref[0] = jnp.broadcast_to(lam_re_ref[...], (SUBLANE, n_cols))
    lam_b_ref[1] = jnp.broadcast_to(lam_im_ref[...], (SUBLANE, n_cols))
    for rg in range(bb // SUBLANE):
        r_off = rg * SUBLANE

        def step(t, h, r_off=r_off):
            rows = pl.ds(pl.multiple_of(t * bb + r_off, SUBLANE), SUBLANE)
            new = []
            for i in range(n_ssm_blocks):
                c_re, c_im, s0 = 2 * bs * i, 2 * bs * i + bs, bs * i
                h_re, h_im = h[2 * i], h[2 * i + 1]
                lre = lam_b_ref[0, :, s0:s0 + bs]
                lim = lam_b_ref[1, :, s0:s0 + bs]
                n_re = lre * h_re - lim * h_im + bu_ref[rows, c_re:c_re + bs]
                n_im = lre * h_im + lim * h_re + bu_ref[rows, c_im:c_im + bs]
                bu_ref[rows, c_re:c_re + bs] = n_re
                bu_ref[rows, c_im:c_im + bs] = n_im
                new += [n_re, n_im]
            return tuple(new)

        h0 = []
        for i in range(n_ssm_blocks):
            h0 += [h_re_ref[r_off:r_off + SUBLANE, bs * i:bs * (i + 1)],
                   h_im_ref[r_off:r_off + SUBLANE, bs * i:bs * (i + 1)]]
        h = lax.fori_loop(0, tc, step, tuple(h0), unroll=min(tc, 8))
        for i in range(n_ssm_blocks):
            h_re_ref[r_off:r_off + SUBLANE, bs * i:bs * (i + 1)] = h[2 * i]
            h_im_ref[r_off:r_off + SUBLANE, bs * i:bs * (i + 1)] = h[2 * i + 1]
    ys = []
    for i in range(n_ssm_blocks):
        cols = slice(i * SSM_BLOCK_CH, (i + 1) * SSM_BLOCK_CH)
        ys.append(_bdot(bu_ref[:, 2 * bs * i:2 * bs * (i + 1)], c_ref[i])
                  + d_skip_ref[:, cols] * u_ssm[:, cols])
    y = jnp.concatenate(ys, axis=-1)
    gl = 0.5 * y * (1.0 + lax.erf(y * _SQRT_HALF))
    y_ssm = gl * jax.nn.sigmoid(_bdot(gl, w_glu_ref[...]) + b_glu_ref[...])
    mix_ref[:, pool_width:] = y_ssm.astype(jnp.bfloat16)

    x = x + jnp.dot(mix_ref[...], w_out_ref[...], preferred_element_type=jnp.float32)

    hn = _rms(x, g_ffn_ref[...]).astype(jnp.bfloat16)
    for j in range(d_ff // FFN_CHUNK):
        c0 = j * FFN_CHUNK
        gate = jnp.dot(hn, w_gu_ref[:, c0:c0 + FFN_CHUNK], preferred_element_type=jnp.float32)
        up = jnp.dot(hn, w_gu_ref[:, d_ff + c0:d_ff + c0 + FFN_CHUNK],
                     preferred_element_type=jnp.float32)
        act_ref[:, c0:c0 + FFN_CHUNK] = (gate * jax.nn.sigmoid(gate) * up).astype(jnp.bfloat16)
    x = x + jnp.dot(act_ref[...], w_down_ref[...], preferred_element_type=jnp.float32)

    gate = jax.nn.sigmoid(_bdot(_rms(x, g_ple_ref[...]), w_ple_gate_ref[...]))
    x = x + _bdot(p, w_ple_ref[...]) * gate
    if final_norm:
        x = _rms(x, g_final_ref[...])

    if carry:
        @pl.when(pid >= 2)
        def _():
            _wait_all(o_copies(pid - 2, slot))

        obuf_ref[slot] = x.reshape(tc, bb, x.shape[-1])
        _start_all(o_copies(pid, slot))

        @pl.when(pid == n_steps - 1)
        def _():
            @pl.when(pid >= 1)
            def _():
                _wait_all(o_copies(pid - 1, 1 - slot))
            _wait_all(o_copies(pid, slot))
    else:
        o_ref[...] = x


def _layer_call(layer, x, p, state, weights, g_final, *, tc, bb, carry, pos0, final_norm, name):
    m = tc * bb
    hb = POOL_BUF * bb
    d_model = x.shape[-1]
    ple_dim = p.shape[-1]
    n_tiles = (x.shape[1] // tc) if carry else (x.shape[0] // m)
    (g_mix, w_in, w_pool, pool_scale, lam_re, lam_im, b_cat, c_cat, d_skip, w_glu, b_glu,
     w_out, g_ffn, w_gu, w_down, g_ple, w_ple, w_ple_gate) = weights
    pool_width = pool_scale.shape[-1]
    n_ssm_blocks = b_cat.shape[1]
    n_state_cols = lam_re.shape[-1]
    d_ff = w_down.shape[1]

    def resident(arr):
        nd = arr.ndim - 1
        return pl.BlockSpec((None,) + arr.shape[1:], lambda i, nd=nd: (layer,) + (0,) * nd,
                            pipeline_mode=pl.Buffered(1))

    hbm = pl.BlockSpec(memory_space=pl.ANY)
    if carry:
        in_specs = [hbm, hbm]
        args = [x, p]
        state_spec = lambda r, c: pl.BlockSpec((r, c), lambda i: (0, 0))
        x_out_spec = hbm
        x_out_shape = jax.ShapeDtypeStruct(x.shape, jnp.float32)
    else:
        in_specs = [pl.BlockSpec((m, d_model), lambda i: (i, 0)),
                    pl.BlockSpec((None, m, ple_dim), lambda i: (layer, i, 0))]
        layer_tile = lambda r, c: pl.BlockSpec((None, r, c), lambda i: (layer, i, 0))
        in_specs += [layer_tile(hb, pool_width), layer_tile(bb, n_state_cols),
                     layer_tile(bb, n_state_cols)]
        args = [x, p] + list(state)
        state_spec = lambda r, c: pl.BlockSpec((r, c), lambda i: (i, 0))
        x_out_spec = pl.BlockSpec((m, d_model), lambda i: (i, 0))
        x_out_shape = jax.ShapeDtypeStruct(x.shape, jnp.float32)
    in_specs += [resident(w) for w in weights]
    in_specs.append(pl.BlockSpec(g_final.shape, lambda i: (0, 0), pipeline_mode=pl.Buffered(1)))
    args += list(weights) + [g_final]

    n_state_rows = bb if carry else n_tiles * bb
    n_hist_rows = hb if carry else n_tiles * hb
    out_shape = [x_out_shape,
                 jax.ShapeDtypeStruct((n_hist_rows, pool_width), jnp.float32),
                 jax.ShapeDtypeStruct((n_state_rows, n_state_cols), jnp.float32),
                 jax.ShapeDtypeStruct((n_state_rows, n_state_cols), jnp.float32)]
    out_specs = [x_out_spec, state_spec(hb, pool_width), state_spec(bb, n_state_cols),
                 state_spec(bb, n_state_cols)]
    scratch = [pltpu.VMEM((hb + m, pool_width), jnp.float32),
               pltpu.VMEM((m, 2 * n_state_cols), jnp.float32),
               pltpu.VMEM((m, w_out.shape[1]), jnp.bfloat16),
               pltpu.VMEM((m, d_ff), jnp.bfloat16),
               pltpu.VMEM((2, SUBLANE, n_state_cols), jnp.float32)]
    if carry:
        scratch += [pltpu.VMEM((2, tc, bb, d_model), jnp.float32),
                    pltpu.VMEM((2, tc, bb, ple_dim), jnp.float32),
                    pltpu.VMEM((2, tc, bb, d_model), jnp.float32),
                    pltpu.SemaphoreType.DMA((2, 2)),
                    pltpu.SemaphoreType.DMA((2,))]
    kern = functools.partial(_layer_kernel, layer=layer, tc=tc, bb=bb, carry=carry, pos0=pos0,
                             final_norm=final_norm, n_ssm_blocks=n_ssm_blocks, d_ff=d_ff)
    return pl.pallas_call(
        kern,
        grid=(n_tiles,),
        in_specs=in_specs,
        out_specs=out_specs,
        out_shape=out_shape,
        scratch_shapes=scratch,
        compiler_params=pltpu.CompilerParams(dimension_semantics=("arbitrary",),
                                             vmem_limit_bytes=VMEM_LIMIT_BYTES),
        name=name,
    )(*args)


PROMPT_TC = 64
SAMPLE_BB = 64


def kernel(x_prompt, x_sample, state_pool, state_ssm_re, state_ssm_im, p_prompt, p_sample, g_mix, w_in, w_pool, pool_scale, ssm_a_re, ssm_a_im, ssm_log_dt, ssm_b_re, ssm_b_im, ssm_c_re, ssm_c_im, ssm_d, w_glu, b_glu, w_out, g_ffn, w_gate_up, w_down, g_ple, w_ple, w_ple_gate, g_final):
    depth = w_in.shape[0]
    batch, seq, d_model = x_prompt.shape
    dec_batch, dec_seq, _ = x_sample.shape
    n_groups, n_state = ssm_a_re.shape[1:]
    n_state_cols = n_groups * n_state
    pool_width = pool_scale.shape[-1]
    n_ssm_blocks = n_groups // SSM_BLOCK_GROUPS
    bf16 = jnp.bfloat16
    assert batch == SUBLANE and seq % PROMPT_TC == 0 and dec_batch % SAMPLE_BB == 0

    lam_re, lam_im, bb_re, bb_im = _ssm_prep(ssm_a_re, ssm_a_im, ssm_log_dt, ssm_b_re, ssm_b_im)

    def b_blocks(bb):
        w = jnp.transpose(bb, (0, 2, 1, 3)).reshape(
            depth, n_ssm_blocks, SSM_BLOCK_GROUPS, SSM_GROUP_DIM, n_state)
        return _block_diag(w)

    def c_blocks(c):
        w = jnp.transpose(c, (0, 1, 3, 2)).reshape(
            depth, n_ssm_blocks, SSM_BLOCK_GROUPS, n_state, SSM_GROUP_DIM)
        return _block_diag(w)

    b_cat = jnp.concatenate([b_blocks(bb_re), b_blocks(bb_im)], axis=-1).astype(bf16)
    c_cat = jnp.concatenate([c_blocks(ssm_c_re), -c_blocks(ssm_c_im)], axis=-2).astype(bf16)
    row = lambda a: a[:, None, :]
    weights = (row(g_mix), w_in.astype(bf16), w_pool.astype(bf16), row(pool_scale),
               lam_re.reshape(depth, 1, n_state_cols), lam_im.reshape(depth, 1, n_state_cols),
               b_cat, c_cat, row(ssm_d), w_glu.astype(bf16), row(b_glu), w_out.astype(bf16),
               row(g_ffn), w_gate_up.astype(bf16), w_down.astype(bf16), row(g_ple),
               w_ple.astype(bf16), w_ple_gate.astype(bf16))
    g_fin = g_final[None, :]

    xp = x_prompt
    hist_p, re_p, im_p = [], [], []
    for l in range(depth):
        xp, hist, h_re, h_im = _layer_call(
            l, xp, p_prompt, None, weights, g_fin, tc=PROMPT_TC, bb=batch, carry=True, pos0=0,
            final_norm=(l == depth - 1), name=f"prompt_layer{l}")
        hist_p.append(hist)
        re_p.append(h_re)
        im_p.append(h_im)
    pool_p = jnp.transpose(jnp.stack(hist_p).reshape(depth, POOL_BUF, batch, pool_width),
                           (0, 2, 1, 3))
    re_p = jnp.stack(re_p).reshape(depth, batch, n_groups, n_state)
    im_p = jnp.stack(im_p).reshape(depth, batch, n_groups, n_state)

    bb = SAMPLE_BB
    nt = dec_batch // bb

    def to_tiles(a):
        lead, (t, w) = a.shape[:-3], a.shape[-2:]
        a = jnp.swapaxes(a.reshape(lead + (nt, bb, t, w)), -3, -2)
        return a.reshape(lead + (nt * t * bb, w))

    def from_tiles(a, t):
        lead, w = a.shape[:-2], a.shape[-1]
        a = jnp.swapaxes(a.reshape(lead + (nt, t, bb, w)), -3, -2)
        return a.reshape(lead + (dec_batch, t, w))

    xs = to_tiles(x_sample)
    state = (to_tiles(state_pool),
             state_ssm_re.reshape(depth, dec_batch, n_state_cols),
             state_ssm_im.reshape(depth, dec_batch, n_state_cols))
    ps = to_tiles(p_sample.astype(bf16))
    hist_s, re_s, im_s = [], [], []
    for l in range(depth):
        xs, hist, h_re, h_im = _layer_call(
            l, xs, ps, state, weights, g_fin, tc=dec_seq, bb=bb, carry=False, pos0=PAST_LEN,
            final_norm=(l == depth - 1), name=f"sample_layer{l}")
        hist_s.append(hist)
        re_s.append(h_re)
        im_s.append(h_im)
    pool_s = from_tiles(jnp.stack(hist_s), POOL_BUF)
    re_s = jnp.stack(re_s).reshape(depth, dec_batch, n_groups, n_state)
    im_s = jnp.stack(im_s).reshape(depth, dec_batch, n_groups, n_state)

    return (xp, from_tiles(xs, dec_seq), pool_p, re_p, im_p, pool_s, re_s, im_s)
```

```python
import functools

import numpy as np
import jax
import jax.numpy as jnp
from jax import lax
from jax.experimental import pallas as pl
from jax.experimental.pallas import tpu as pltpu

LANE = 128
SUBLANE = 8
VMEM_LIMIT_BYTES = 60 * 1024 * 1024

POOL_WINDOWS = (2, 4, 8, 16)
POOL_BUF = max(POOL_WINDOWS) - 1
SSM_GROUP_DIM = 16
SSM_STATE = 64
EPS = 1e-6
PAST_LEN = 16384

SSM_BLOCK_CH = LANE
SSM_BLOCK_GROUPS = SSM_BLOCK_CH // SSM_GROUP_DIM
SSM_BLOCK_STATES = SSM_BLOCK_GROUPS * SSM_STATE

FFN_CHUNK = 256

_SQRT_HALF = np.float32(np.sqrt(0.5))


def _rms(x, g):
    y = x * lax.rsqrt(jnp.mean(x * x, axis=-1, keepdims=True) + EPS)
    return y * g


def _bdot(a, b):
    return jnp.dot(a.astype(jnp.bfloat16), b, preferred_element_type=jnp.float32)


def _ssm_prep_kernel(a_re_ref, a_im_ref, log_dt_ref, b_re_ref, b_im_ref,
                     lam_re_ref, lam_im_ref, bb_re_ref, bb_im_ref):
    ar = a_re_ref[...]
    ai = a_im_ref[...]
    dt = jnp.exp(log_dt_ref[...])
    mag = jnp.exp(dt * ar)
    lam_re = mag * jnp.cos(dt * ai)
    lam_im = mag * jnp.sin(dt * ai)
    den = ar * ar + ai * ai
    nr = lam_re - 1.0
    k_re = (nr * ar + lam_im * ai) / den
    k_im = (lam_im * ar - nr * ai) / den
    lam_re_ref[...] = lam_re
    lam_im_ref[...] = lam_im
    for c in range(SSM_GROUP_DIM):
        br = b_re_ref[c]
        bi = b_im_ref[c]
        bb_re_ref[c] = k_re * br - k_im * bi
        bb_im_ref[c] = k_re * bi + k_im * br


def _ssm_prep(a_re, a_im, log_dt, b_re, b_im):
    depth, n_groups, n_state = a_re.shape
    b_re_t = jnp.transpose(b_re, (0, 3, 1, 2))
    b_im_t = jnp.transpose(b_im, (0, 3, 1, 2))
    gp = pl.BlockSpec((None, n_groups, n_state), lambda l: (l, 0, 0))
    cgp = pl.BlockSpec((None, SSM_GROUP_DIM, n_groups, n_state), lambda l: (l, 0, 0, 0))
    return pl.pallas_call(
        _ssm_prep_kernel,
        grid=(depth,),
        in_specs=[gp, gp, pl.BlockSpec((None, n_groups, 1), lambda l: (l, 0, 0)), cgp, cgp],
        out_specs=[gp, gp, cgp, cgp],
        out_shape=[jax.ShapeDtypeStruct(a_re.shape, jnp.float32)] * 2
        + [jax.ShapeDtypeStruct(b_re_t.shape, jnp.float32)] * 2,
        name="ssm_prep",
    )(a_re, a_im, log_dt[..., None], b_re_t, b_im_t)


def _block_diag(w):
    depth, nb, gb, r, c = w.shape
    eye = jnp.eye(gb, dtype=w.dtype)
    out = w[:, :, :, :, None, :] * eye[None, None, :, None, :, None]
    return out.reshape(depth, nb, gb * r, gb * c)


def _start_all(copies):
    for c in copies:
        c.start()


def _wait_all(copies):
    for c in copies:
        c.wait()


def _layer_kernel(*refs, layer, tc, bb, carry, pos0, final_norm, n_ssm_blocks, d_ff):
    it = iter(refs)
    x_ref = next(it)
    p_ref = next(it)
    if not carry:
        hist_in_ref = next(it)
        h_re_in_ref = next(it)
        h_im_in_ref = next(it)
    (g_mix_ref, w_in_ref, w_pool_ref, pool_scale_ref, lam_re_ref, lam_im_ref,
     b_ref, c_ref, d_skip_ref, w_glu_ref, b_glu_ref, w_out_ref, g_ffn_ref,
     w_gu_ref, w_down_ref, g_ple_ref, w_ple_ref, w_ple_gate_ref, g_final_ref) = (
         next(it) for _ in range(19))
    o_ref = next(it)
    hist_out_ref = next(it)
    h_re_ref = next(it)
    h_im_ref = next(it)
    ue_ref = next(it)
    bu_ref = next(it)
    mix_ref = next(it)
    act_ref = next(it)
    lam_b_ref = next(it)
    if carry:
        xbuf_ref = next(it)
        pbuf_ref = next(it)
        obuf_ref = next(it)
        x_sem = next(it)
        p_sem = next(it)
        o_sem = next(it)

    m = tc * bb
    hb = POOL_BUF * bb
    pool_width = ue_ref.shape[1]
    bs = SSM_BLOCK_STATES
    n_cols = n_ssm_blocks * bs
    pid = pl.program_id(0)

    def stage_a(x, a_slot, tile):
        z = _bdot(_rms(x, g_mix_ref[...]), w_in_ref[...])
        ue_ref[hb:hb + m, :] = z[:, :pool_width]
        u_ssm = z[:, pool_width:]

        t_local = lax.shift_right_logical(
            lax.broadcasted_iota(jnp.int32, (m, LANE), 0), int(np.log2(bb)))
        pos = t_local + (pos0 + tile * tc)
        for g, w in enumerate(POOL_WINDOWS):
            cols = slice(g * LANE, (g + 1) * LANE)
            cur = ue_ref[hb:hb + m, cols]
            s = cur
            for j in range(1, w):
                s = s + ue_ref[hb - j * bb:hb - j * bb + m, cols]
            cnt = jnp.minimum(pos + 1, w).astype(jnp.float32)
            y = _bdot(s / cnt - cur, w_pool_ref[g]) * pool_scale_ref[:, cols]
            mix_ref[a_slot, :, cols] = y.astype(jnp.bfloat16)
        hist_out_ref[...] = ue_ref[m:m + hb, :]
        if carry:
            ue_ref[0:hb, :] = ue_ref[m:m + hb, :]

        for i in range(n_ssm_blocks):
            u_i = u_ssm[:, i * SSM_BLOCK_CH:(i + 1) * SSM_BLOCK_CH]
            bu_ref[:, 2 * bs * i:2 * bs * (i + 1)] = _bdot(u_i, b_ref[i])
        lam_b_ref[0] = jnp.broadcast_to(lam_re_ref[...], (SUBLANE, n_cols))
        lam_b_ref[1] = jnp.broadcast_to(lam_im_ref[...], (SUBLANE, n_cols))
        for rg in range(bb // SUBLANE):
            r_off = rg * SUBLANE

            def step(t, h, r_off=r_off):
                rows = pl.ds(pl.multiple_of(t * bb + r_off, SUBLANE), SUBLANE)
                new = []
                for i in range(n_ssm_blocks):
                    c_re, c_im, s0 = 2 * bs * i, 2 * bs * i + bs, bs * i
                    h_re, h_im = h[2 * i], h[2 * i + 1]
                    lre = lam_b_ref[0, :, s0:s0 + bs]
                    lim = lam_b_ref[1, :, s0:s0 + bs]
                    n_re = lre * h_re - lim * h_im + bu_ref[rows, c_re:c_re + bs]
                    n_im = lre * h_im + lim * h_re + bu_ref[rows, c_im:c_im + bs]
                    bu_ref[rows, c_re:c_re + bs] = n_re
                    bu_ref[rows, c_im:c_im + bs] = n_im
                    new += [n_re, n_im]
                return tuple(new)

            h0 = []
            for i in range(n_ssm_blocks):
                h0 += [h_re_ref[r_off:r_off + SUBLANE, bs * i:bs * (i + 1)],
                       h_im_ref[r_off:r_off + SUBLANE, bs * i:bs * (i + 1)]]
            h = lax.fori_loop(0, tc, step, tuple(h0), unroll=True if carry else min(tc, 8))
            for i in range(n_ssm_blocks):
                h_re_ref[r_off:r_off + SUBLANE, bs * i:bs * (i + 1)] = h[2 * i]
                h_im_ref[r_off:r_off + SUBLANE, bs * i:bs * (i + 1)] = h[2 * i + 1]
        ys = []
        for i in range(n_ssm_blocks):
            cols = slice(i * SSM_BLOCK_CH, (i + 1) * SSM_BLOCK_CH)
            ys.append(_bdot(bu_ref[:, 2 * bs * i:2 * bs * (i + 1)], c_ref[i])
                      + d_skip_ref[:, cols] * u_ssm[:, cols])
        y = jnp.concatenate(ys, axis=-1)
        gl = 0.5 * y * (1.0 + lax.erf(y * _SQRT_HALF))
        y_ssm = gl * jax.nn.sigmoid(_bdot(gl, w_glu_ref[...]) + b_glu_ref[...])
        mix_ref[a_slot, :, pool_width:] = y_ssm.astype(jnp.bfloat16)

    def stage_b(x, p, b_slot):
        x = x + jnp.dot(mix_ref[b_slot], w_out_ref[...], preferred_element_type=jnp.float32)
        hn = _rms(x, g_ffn_ref[...]).astype(jnp.bfloat16)
        for j in range(d_ff // FFN_CHUNK):
            c0 = j * FFN_CHUNK
            gate = jnp.dot(hn, w_gu_ref[:, c0:c0 + FFN_CHUNK],
                           preferred_element_type=jnp.float32)
            up = jnp.dot(hn, w_gu_ref[:, d_ff + c0:d_ff + c0 + FFN_CHUNK],
                         preferred_element_type=jnp.float32)
            act_ref[:, c0:c0 + FFN_CHUNK] = (
                gate * jax.nn.sigmoid(gate) * up).astype(jnp.bfloat16)
        x = x + jnp.dot(act_ref[...], w_down_ref[...], preferred_element_type=jnp.float32)
        gate = jax.nn.sigmoid(_bdot(_rms(x, g_ple_ref[...]), w_ple_gate_ref[...]))
        x = x + _bdot(p, w_ple_ref[...]) * gate
        if final_norm:
            x = _rms(x, g_final_ref[...])
        return x

    if not carry:
        ue_ref[0:hb, :] = hist_in_ref[...]
        h_re_ref[...] = h_re_in_ref[...]
        h_im_ref[...] = h_im_in_ref[...]
        x = x_ref[...]
        stage_a(x, 0, 0)
        o_ref[...] = stage_b(x, p_ref[...], 0)
        return

    n = pl.num_programs(0) - 1

    def x_copies(tile):
        s = lax.rem(tile, 3)
        return [pltpu.make_async_copy(x_ref.at[b, pl.ds(tile * tc, tc), :],
                                      xbuf_ref.at[s, :, b, :], x_sem.at[s]) for b in range(bb)]

    def p_copies(tile):
        s = lax.rem(tile, 2)
        return [pltpu.make_async_copy(p_ref.at[layer, b, pl.ds(tile * tc, tc), :],
                                      pbuf_ref.at[s, :, b, :], p_sem.at[s]) for b in range(bb)]

    def o_copies(tile):
        s = lax.rem(tile, 2)
        return [pltpu.make_async_copy(obuf_ref.at[s, :, b, :],
                                      o_ref.at[b, pl.ds(tile * tc, tc), :], o_sem.at[s])
                for b in range(bb)]

    def x_tile(tile):
        return xbuf_ref[lax.rem(tile, 3)].reshape(m, xbuf_ref.shape[-1])

    def run_b(tile):
        _wait_all(p_copies(tile))
        out = stage_b(x_tile(tile), pbuf_ref[lax.rem(tile, 2)].reshape(m, pbuf_ref.shape[-1]),
                      lax.rem(tile, 2))
        obuf_ref[lax.rem(tile, 2)] = out.reshape(tc, bb, out.shape[-1])
        _start_all(o_copies(tile))

    @pl.when(pid == 0)
    def _():
        _start_all(x_copies(0))
        ue_ref[0:hb, :] = jnp.zeros((hb, pool_width), jnp.float32)
        h_re_ref[...] = jnp.zeros(h_re_ref.shape, jnp.float32)
        h_im_ref[...] = jnp.zeros(h_im_ref.shape, jnp.float32)

    @pl.when(pid + 1 < n)
    def _():
        _start_all(x_copies(pid + 1))

    @pl.when(pid < n)
    def _():
        _start_all(p_copies(pid))

    @pl.when(pid >= 3)
    def _():
        _wait_all(o_copies(pid - 3))

    @pl.when(pid == 0)
    def _():
        _wait_all(x_copies(0))
        stage_a(x_tile(0), 0, 0)

    @pl.when((pid > 0) & (pid < n))
    def _():
        _wait_all(x_copies(pid))
        stage_a(x_tile(pid), lax.rem(pid, 2), pid)
        run_b(pid - 1)

    @pl.when(pid == n)
    def _():
        run_b(n - 1)

        @pl.when(n >= 2)
        def _():
            _wait_all(o_copies(n - 2))

        _wait_all(o_copies(n - 1))


def _layer_call(layer, x, p, state, weights, g_final, *, tc, bb, carry, pos0, final_norm, name):
    m = tc * bb
    hb = POOL_BUF * bb
    d_model = x.shape[-1]
    ple_dim = p.shape[-1]
    n_tiles = (x.shape[1] // tc) if carry else (x.shape[0] // m)
    (g_mix, w_in, w_pool, pool_scale, lam_re, lam_im, b_cat, c_cat, d_skip, w_glu, b_glu,
     w_out, g_ffn, w_gu, w_down, g_ple, w_ple, w_ple_gate) = weights
    pool_width = pool_scale.shape[-1]
    n_ssm_blocks = b_cat.shape[1]
    n_state_cols = lam_re.shape[-1]
    d_ff = w_down.shape[1]

    def resident(arr):
        nd = arr.ndim - 1
        return pl.BlockSpec((None,) + arr.shape[1:], lambda i, nd=nd: (layer,) + (0,) * nd,
                            pipeline_mode=pl.Buffered(1))

    hbm = pl.BlockSpec(memory_space=pl.ANY)
    if carry:
        in_specs = [hbm, hbm]
        args = [x, p]
        state_spec = lambda r, c: pl.BlockSpec((r, c), lambda i: (0, 0))
        x_out_spec = hbm
    else:
        in_specs = [pl.BlockSpec((m, d_model), lambda i: (i, 0)),
                    pl.BlockSpec((None, m, ple_dim), lambda i: (layer, i, 0))]
        layer_tile = lambda r, c: pl.BlockSpec((None, r, c), lambda i: (layer, i, 0))
        in_specs += [layer_tile(hb, pool_width), layer_tile(bb, n_state_cols),
                     layer_tile(bb, n_state_cols)]
        args = [x, p] + list(state)
        state_spec = lambda r, c: pl.BlockSpec((r, c), lambda i: (i, 0))
        x_out_spec = pl.BlockSpec((m, d_model), lambda i: (i, 0))
    in_specs += [resident(w) for w in weights]
    in_specs.append(pl.BlockSpec(g_final.shape, lambda i: (0, 0), pipeline_mode=pl.Buffered(1)))
    args += list(weights) + [g_final]

    n_state_rows = bb if carry else n_tiles * bb
    n_hist_rows = hb if carry else n_tiles * hb
    out_shape = [jax.ShapeDtypeStruct(x.shape, jnp.float32),
                 jax.ShapeDtypeStruct((n_hist_rows, pool_width), jnp.float32),
                 jax.ShapeDtypeStruct((n_state_rows, n_state_cols), jnp.float32),
                 jax.ShapeDtypeStruct((n_state_rows, n_state_cols), jnp.float32)]
    out_specs = [x_out_spec, state_spec(hb, pool_width), state_spec(bb, n_state_cols),
                 state_spec(bb, n_state_cols)]
    scratch = [pltpu.VMEM((hb + m, pool_width), jnp.float32),
               pltpu.VMEM((m, 2 * n_state_cols), jnp.float32),
               pltpu.VMEM((2 if carry else 1, m, w_out.shape[1]), jnp.bfloat16),
               pltpu.VMEM((m, d_ff), jnp.bfloat16),
               pltpu.VMEM((2, SUBLANE, n_state_cols), jnp.float32)]
    if carry:
        scratch += [pltpu.VMEM((3, tc, bb, d_model), jnp.float32),
                    pltpu.VMEM((2, tc, bb, ple_dim), jnp.float32),
                    pltpu.VMEM((2, tc, bb, d_model), jnp.float32),
                    pltpu.SemaphoreType.DMA((3,)),
                    pltpu.SemaphoreType.DMA((2,)),
                    pltpu.SemaphoreType.DMA((2,))]
    kern = functools.partial(_layer_kernel, layer=layer, tc=tc, bb=bb, carry=carry, pos0=pos0,
                             final_norm=final_norm, n_ssm_blocks=n_ssm_blocks, d_ff=d_ff)
    return pl.pallas_call(
        kern,
        grid=(n_tiles + 1 if carry else n_tiles,),
        in_specs=in_specs,
        out_specs=out_specs,
        out_shape=out_shape,
        scratch_shapes=scratch,
        compiler_params=pltpu.CompilerParams(dimension_semantics=("arbitrary",),
                                             vmem_limit_bytes=VMEM_LIMIT_BYTES),
        name=name,
    )(*args)


PROMPT_TC = 64
SAMPLE_BB = 64


def kernel(x_prompt, x_sample, state_pool, state_ssm_re, state_ssm_im, p_prompt, p_sample, g_mix, w_in, w_pool, pool_scale, ssm_a_re, ssm_a_im, ssm_log_dt, ssm_b_re, ssm_b_im, ssm_c_re, ssm_c_im, ssm_d, w_glu, b_glu, w_out, g_ffn, w_gate_up, w_down, g_ple, w_ple, w_ple_gate, g_final):
    depth = w_in.shape[0]
    batch, seq, d_model = x_prompt.shape
    dec_batch, dec_seq, _ = x_sample.shape
    n_groups, n_state = ssm_a_re.shape[1:]
    n_state_cols = n_groups * n_state
    pool_width = pool_scale.shape[-1]
    n_ssm_blocks = n_groups // SSM_BLOCK_GROUPS
    bf16 = jnp.bfloat16
    assert batch == SUBLANE and seq % PROMPT_TC == 0 and dec_batch % SAMPLE_BB == 0

    lam_re, lam_im, bb_re, bb_im = _ssm_prep(ssm_a_re, ssm_a_im, ssm_log_dt, ssm_b_re, ssm_b_im)

    def b_blocks(bb):
        w = jnp.transpose(bb, (0, 2, 1, 3)).reshape(
            depth, n_ssm_blocks, SSM_BLOCK_GROUPS, SSM_GROUP_DIM, n_state)
        return _block_diag(w)

    def c_blocks(c):
        w = jnp.transpose(c, (0, 1, 3, 2)).reshape(
            depth, n_ssm_blocks, SSM_BLOCK_GROUPS, n_state, SSM_GROUP_DIM)
        return _block_diag(w)

    b_cat = jnp.concatenate([b_blocks(bb_re), b_blocks(bb_im)], axis=-1).astype(bf16)
    c_cat = jnp.concatenate([c_blocks(ssm_c_re), -c_blocks(ssm_c_im)], axis=-2).astype(bf16)
    row = lambda a: a[:, None, :]
    weights = (row(g_mix), w_in.astype(bf16), w_pool.astype(bf16), row(pool_scale),
               lam_re.reshape(depth, 1, n_state_cols), lam_im.reshape(depth, 1, n_state_cols),
               b_cat, c_cat, row(ssm_d), w_glu.astype(bf16), row(b_glu), w_out.astype(bf16),
               row(g_ffn), w_gate_up.astype(bf16), w_down.astype(bf16), row(g_ple),
               w_ple.astype(bf16), w_ple_gate.astype(bf16))
    g_fin = g_final[None, :]

    xp = x_prompt
    hist_p, re_p, im_p = [], [], []
    for l in range(depth):
        xp, hist, h_re, h_im = _layer_call(
            l, xp, p_prompt, None, weights, g_fin, tc=PROMPT_TC, bb=batch, carry=True, pos0=0,
            final_norm=(l == depth - 1), name=f"prompt_layer{l}")
        hist_p.append(hist)
        re_p.append(h_re)
        im_p.append(h_im)
    pool_p = jnp.transpose(jnp.stack(hist_p).reshape(depth, POOL_BUF, batch, pool_width),
                           (0, 2, 1, 3))
    re_p = jnp.stack(re_p).reshape(depth, batch, n_groups, n_state)
    im_p = jnp.stack(im_p).reshape(depth, batch, n_groups, n_state)

    bb = SAMPLE_BB
    nt = dec_batch // bb

    def to_tiles(a):
        lead, (t, w) = a.shape[:-3], a.shape[-2:]
        a = jnp.swapaxes(a.reshape(lead + (nt, bb, t, w)), -3, -2)
        return a.reshape(lead + (nt * t * bb, w))

    def from_tiles(a, t):
        lead, w = a.shape[:-2], a.shape[-1]
        a = jnp.swapaxes(a.reshape(lead + (nt, t, bb, w)), -3, -2)
        return a.reshape(lead + (dec_batch, t, w))

    xs = to_tiles(x_sample)
    state = (to_tiles(state_pool),
             state_ssm_re.reshape(depth, dec_batch, n_state_cols),
             state_ssm_im.reshape(depth, dec_batch, n_state_cols))
    ps = to_tiles(p_sample.astype(bf16))
    hist_s, re_s, im_s = [], [], []
    for l in range(depth):
        xs, hist, h_re, h_im = _layer_call(
            l, xs, ps, state, weights, g_fin, tc=dec_seq, bb=bb, carry=False, pos0=PAST_LEN,
            final_norm=(l == depth - 1), name=f"sample_layer{l}")
        hist_s.append(hist)
        re_s.append(h_re)
        im_s.append(h_im)
    pool_s = from_tiles(jnp.stack(hist_s), POOL_BUF)
    re_s = jnp.stack(re_s).reshape(depth, dec_batch, n_groups, n_state)
    im_s = jnp.stack(im_s).reshape(depth, dec_batch, n_groups, n_state)

    return (xp, from_tiles(xs, dec_seq), pool_p, re_p, im_p, pool_s, re_s, im_s)
```

```python
import functools

import numpy as np
import jax
import jax.numpy as jnp
from jax import lax
from jax.experimental import pallas as pl
from jax.experimental.pallas import tpu as pltpu

LANE = 128
SUBLANE = 8
VMEM_LIMIT_BYTES = 60 * 1024 * 1024

POOL_WINDOWS = (2, 4, 8, 16)
POOL_BUF = max(POOL_WINDOWS) - 1
SSM_GROUP_DIM = 16
SSM_STATE = 64
EPS = 1e-6
PAST_LEN = 16384

SSM_BLOCK_CH = LANE
SSM_BLOCK_GROUPS = SSM_BLOCK_CH // SSM_GROUP_DIM
SSM_BLOCK_STATES = SSM_BLOCK_GROUPS * SSM_STATE

FFN_CHUNK = 256

_SQRT_HALF = np.float32(np.sqrt(0.5))


def _rms(x, g):
    y = x * lax.rsqrt(jnp.mean(x * x, axis=-1, keepdims=True) + EPS)
    return y * g


def _bdot(a, b):
    return jnp.dot(a.astype(jnp.bfloat16), b, preferred_element_type=jnp.float32)


def _ssm_prep_kernel(a_re_ref, a_im_ref, log_dt_ref, b_re_ref, b_im_ref,
                     lam_re_ref, lam_im_ref, bb_re_ref, bb_im_ref):
    ar = a_re_ref[...]
    ai = a_im_ref[...]
    dt = jnp.exp(log_dt_ref[...])
    mag = jnp.exp(dt * ar)
    lam_re = mag * jnp.cos(dt * ai)
    lam_im = mag * jnp.sin(dt * ai)
    den = ar * ar + ai * ai
    nr = lam_re - 1.0
    k_re = (nr * ar + lam_im * ai) / den
    k_im = (lam_im * ar - nr * ai) / den
    lam_re_ref[...] = lam_re
    lam_im_ref[...] = lam_im
    for c in range(SSM_GROUP_DIM):
        br = b_re_ref[c]
        bi = b_im_ref[c]
        bb_re_ref[c] = k_re * br - k_im * bi
        bb_im_ref[c] = k_re * bi + k_im * br


def _ssm_prep(a_re, a_im, log_dt, b_re, b_im):
    depth, n_groups, n_state = a_re.shape
    b_re_t = jnp.transpose(b_re, (0, 3, 1, 2))
    b_im_t = jnp.transpose(b_im, (0, 3, 1, 2))
    gp = pl.BlockSpec((None, n_groups, n_state), lambda l: (l, 0, 0))
    cgp = pl.BlockSpec((None, SSM_GROUP_DIM, n_groups, n_state), lambda l: (l, 0, 0, 0))
    return pl.pallas_call(
        _ssm_prep_kernel,
        grid=(depth,),
        in_specs=[gp, gp, pl.BlockSpec((None, n_groups, 1), lambda l: (l, 0, 0)), cgp, cgp],
        out_specs=[gp, gp, cgp, cgp],
        out_shape=[jax.ShapeDtypeStruct(a_re.shape, jnp.float32)] * 2
        + [jax.ShapeDtypeStruct(b_re_t.shape, jnp.float32)] * 2,
        name="ssm_prep",
    )(a_re, a_im, log_dt[..., None], b_re_t, b_im_t)


def _block_diag(w, gb):
    rows, c = w.shape[-2:]
    r = rows // gb
    same_block = (jnp.arange(rows)[:, None] // r) == (jnp.arange(gb * c)[None, :] // c)
    return jnp.where(same_block, jnp.tile(w, (1, 1, 1, gb)), 0.0)


def _start_all(copies):
    for c in copies:
        c.start()


def _wait_all(copies):
    for c in copies:
        c.wait()


def _layer_kernel(*refs, layer, tc, bb, carry, pos0, final_norm, n_ssm_blocks, d_ff):
    it = iter(refs)
    x_ref = next(it)
    p_ref = next(it)
    if not carry:
        hist_in_ref = next(it)
        h_re_in_ref = next(it)
        h_im_in_ref = next(it)
    (g_mix_ref, w_in_ref, w_pool_ref, pool_scale_ref, lam_re_ref, lam_im_ref,
     b_ref, c_ref, d_skip_ref, w_glu_ref, b_glu_ref, w_out_ref, g_ffn_ref,
     w_gu_ref, w_down_ref, g_ple_ref, w_ple_ref, w_ple_gate_ref, g_final_ref) = (
         next(it) for _ in range(19))
    o_ref = next(it)
    hist_out_ref = next(it)
    h_re_ref = next(it)
    h_im_ref = next(it)
    ue_ref = next(it)
    bu_ref = next(it)
    mix_ref = next(it)
    act_ref = next(it)
    lam_b_ref = next(it)
    if carry:
        xbuf_ref = next(it)
        pbuf_ref = next(it)
        obuf_ref = next(it)
        x_sem = next(it)
        p_sem = next(it)
        o_sem = next(it)
    late_hbm = (w_out_ref, w_gu_ref, w_down_ref, w_ple_ref, w_ple_gate_ref)
    late_buf = tuple(next(it) for _ in late_hbm)
    w_sem = next(it)
    w_out_ref, w_gu_ref, w_down_ref, w_ple_ref, w_ple_gate_ref = late_buf

    def late_copies():
        return [pltpu.make_async_copy(h.at[layer], v, w_sem.at[i])
                for i, (h, v) in enumerate(zip(late_hbm, late_buf))]

    m = tc * bb
    hb = POOL_BUF * bb
    pool_width = ue_ref.shape[1]
    bs = SSM_BLOCK_STATES
    n_cols = n_ssm_blocks * bs
    pid = pl.program_id(0)

    def stage_a(x, a_slot, tile):
        z = _bdot(_rms(x, g_mix_ref[...]), w_in_ref[...])
        ue_ref[hb:hb + m, :] = z[:, :pool_width]
        u_ssm = z[:, pool_width:]

        t_local = lax.shift_right_logical(
            lax.broadcasted_iota(jnp.int32, (m, LANE), 0), int(np.log2(bb)))
        pos = t_local + (pos0 + tile * tc)
        for g, w in enumerate(POOL_WINDOWS):
            cols = slice(g * LANE, (g + 1) * LANE)
            cur = ue_ref[hb:hb + m, cols]
            s = cur
            for j in range(1, w):
                s = s + ue_ref[hb - j * bb:hb - j * bb + m, cols]
            cnt = jnp.minimum(pos + 1, w).astype(jnp.float32)
            y = _bdot(s / cnt - cur, w_pool_ref[g]) * pool_scale_ref[:, cols]
            mix_ref[a_slot, :, cols] = y.astype(jnp.bfloat16)
        hist_out_ref[...] = ue_ref[m:m + hb, :]
        if carry:
            ue_ref[0:hb, :] = ue_ref[m:m + hb, :]

        for i in range(n_ssm_blocks):
            u_i = u_ssm[:, i * SSM_BLOCK_CH:(i + 1) * SSM_BLOCK_CH]
            bu_ref[:, 2 * bs * i:2 * bs * (i + 1)] = _bdot(u_i, b_ref[i])
        lam_b_ref[0] = jnp.broadcast_to(lam_re_ref[...], (SUBLANE, n_cols))
        lam_b_ref[1] = jnp.broadcast_to(lam_im_ref[...], (SUBLANE, n_cols))
        for rg in range(bb // SUBLANE):
            r_off = rg * SUBLANE

            def step(t, h, r_off=r_off):
                rows = pl.ds(pl.multiple_of(t * bb + r_off, SUBLANE), SUBLANE)
                new = []
                for i in range(n_ssm_blocks):
                    c_re, c_im, s0 = 2 * bs * i, 2 * bs * i + bs, bs * i
                    h_re, h_im = h[2 * i], h[2 * i + 1]
                    lre = lam_b_ref[0, :, s0:s0 + bs]
                    lim = lam_b_ref[1, :, s0:s0 + bs]
                    n_re = lre * h_re - lim * h_im + bu_ref[rows, c_re:c_re + bs]
                    n_im = lre * h_im + lim * h_re + bu_ref[rows, c_im:c_im + bs]
                    bu_ref[rows, c_re:c_re + bs] = n_re
                    bu_ref[rows, c_im:c_im + bs] = n_im
                    new += [n_re, n_im]
                return tuple(new)

            h0 = []
            for i in range(n_ssm_blocks):
                h0 += [h_re_ref[r_off:r_off + SUBLANE, bs * i:bs * (i + 1)],
                       h_im_ref[r_off:r_off + SUBLANE, bs * i:bs * (i + 1)]]
            h = lax.fori_loop(0, tc, step, tuple(h0), unroll=True if carry else min(tc, 8))
            for i in range(n_ssm_blocks):
                h_re_ref[r_off:r_off + SUBLANE, bs * i:bs * (i + 1)] = h[2 * i]
                h_im_ref[r_off:r_off + SUBLANE, bs * i:bs * (i + 1)] = h[2 * i + 1]
        ys = []
        for i in range(n_ssm_blocks):
            cols = slice(i * SSM_BLOCK_CH, (i + 1) * SSM_BLOCK_CH)
            ys.append(_bdot(bu_ref[:, 2 * bs * i:2 * bs * (i + 1)], c_ref[i])
                      + d_skip_ref[:, cols] * u_ssm[:, cols])
        y = jnp.concatenate(ys, axis=-1)
        gl = 0.5 * y * (1.0 + lax.erf(y * _SQRT_HALF))
        y_ssm = gl * jax.nn.sigmoid(_bdot(gl, w_glu_ref[...]) + b_glu_ref[...])
        mix_ref[a_slot, :, pool_width:] = y_ssm.astype(jnp.bfloat16)

    def stage_b(x, p, b_slot):
        x = x + jnp.dot(mix_ref[b_slot], w_out_ref[...], preferred_element_type=jnp.float32)
        hn = _rms(x, g_ffn_ref[...]).astype(jnp.bfloat16)
        for j in range(d_ff // FFN_CHUNK):
            c0 = j * FFN_CHUNK
            gate = jnp.dot(hn, w_gu_ref[:, c0:c0 + FFN_CHUNK],
                           preferred_element_type=jnp.float32)
            up = jnp.dot(hn, w_gu_ref[:, d_ff + c0:d_ff + c0 + FFN_CHUNK],
                         preferred_element_type=jnp.float32)
            act_ref[:, c0:c0 + FFN_CHUNK] = (
                gate * jax.nn.sigmoid(gate) * up).astype(jnp.bfloat16)
        x = x + jnp.dot(act_ref[...], w_down_ref[...], preferred_element_type=jnp.float32)
        gate = jax.nn.sigmoid(_bdot(_rms(x, g_ple_ref[...]), w_ple_gate_ref[...]))
        x = x + _bdot(p, w_ple_ref[...]) * gate
        if final_norm:
            x = _rms(x, g_final_ref[...])
        return x

    if not carry:
        @pl.when(pid == 0)
        def _():
            _start_all(late_copies())

        ue_ref[0:hb, :] = hist_in_ref[...]
        h_re_ref[...] = h_re_in_ref[...]
        h_im_ref[...] = h_im_in_ref[...]
        x = x_ref[...]
        stage_a(x, 0, 0)

        @pl.when(pid == 0)
        def _():
            _wait_all(late_copies())

        o_ref[...] = stage_b(x, p_ref[...], 0)
        return

    n = pl.num_programs(0) - 1

    def x_copies(tile):
        s = lax.rem(tile, 3)
        return [pltpu.make_async_copy(x_ref.at[b, pl.ds(tile * tc, tc), :],
                                      xbuf_ref.at[s, :, b, :], x_sem.at[s]) for b in range(bb)]

    def p_copies(tile):
        s = lax.rem(tile, 2)
        return [pltpu.make_async_copy(p_ref.at[layer, b, pl.ds(tile * tc, tc), :],
                                      pbuf_ref.at[s, :, b, :], p_sem.at[s]) for b in range(bb)]

    def o_copies(tile):
        s = lax.rem(tile, 2)
        return [pltpu.make_async_copy(obuf_ref.at[s, :, b, :],
                                      o_ref.at[b, pl.ds(tile * tc, tc), :], o_sem.at[s])
                for b in range(bb)]

    def x_tile(tile):
        return xbuf_ref[lax.rem(tile, 3)].reshape(m, xbuf_ref.shape[-1])

    def run_b(tile):
        _wait_all(p_copies(tile))
        out = stage_b(x_tile(tile), pbuf_ref[lax.rem(tile, 2)].reshape(m, pbuf_ref.shape[-1]),
                      lax.rem(tile, 2))
        obuf_ref[lax.rem(tile, 2)] = out.reshape(tc, bb, out.shape[-1])
        _start_all(o_copies(tile))

    @pl.when(pid == 0)
    def _():
        _start_all(x_copies(0) + late_copies())
        ue_ref[0:hb, :] = jnp.zeros((hb, pool_width), jnp.float32)
        h_re_ref[...] = jnp.zeros(h_re_ref.shape, jnp.float32)
        h_im_ref[...] = jnp.zeros(h_im_ref.shape, jnp.float32)

    @pl.when(pid + 1 < n)
    def _():
        _start_all(x_copies(pid + 1))

    @pl.when(pid < n)
    def _():
        _start_all(p_copies(pid))

    @pl.when(pid >= 3)
    def _():
        _wait_all(o_copies(pid - 3))

    @pl.when(pid == 1)
    def _():
        _wait_all(late_copies())

    @pl.when(pid == 0)
    def _():
        _wait_all(x_copies(0))
        stage_a(x_tile(0), 0, 0)

    @pl.when((pid > 0) & (pid < n))
    def _():
        _wait_all(x_copies(pid))
        stage_a(x_tile(pid), lax.rem(pid, 2), pid)
        run_b(pid - 1)

    @pl.when(pid == n)
    def _():
        run_b(n - 1)

        @pl.when(n >= 2)
        def _():
            _wait_all(o_copies(n - 2))

        _wait_all(o_copies(n - 1))


def _layer_call(layer, x, p, state, weights, g_final, *, tc, bb, carry, pos0, final_norm, name):
    m = tc * bb
    hb = POOL_BUF * bb
    d_model = x.shape[-1]
    ple_dim = p.shape[-1]
    n_tiles = (x.shape[1] // tc) if carry else (x.shape[0] // m)
    (g_mix, w_in, w_pool, pool_scale, lam_re, lam_im, b_cat, c_cat, d_skip, w_glu, b_glu,
     w_out, g_ffn, w_gu, w_down, g_ple, w_ple, w_ple_gate) = weights
    pool_width = pool_scale.shape[-1]
    n_ssm_blocks = b_cat.shape[1]
    n_state_cols = lam_re.shape[-1]
    d_ff = w_down.shape[1]

    def resident(arr):
        nd = arr.ndim - 1
        return pl.BlockSpec((None,) + arr.shape[1:], lambda i, nd=nd: (layer,) + (0,) * nd,
                            pipeline_mode=pl.Buffered(1))

    hbm = pl.BlockSpec(memory_space=pl.ANY)
    if carry:
        in_specs = [hbm, hbm]
        args = [x, p]
        state_spec = lambda r, c: pl.BlockSpec((r, c), lambda i: (0, 0))
        x_out_spec = hbm
    else:
        in_specs = [pl.BlockSpec((m, d_model), lambda i: (i, 0)),
                    pl.BlockSpec((None, m, ple_dim), lambda i: (layer, i, 0))]
        layer_tile = lambda r, c: pl.BlockSpec((None, r, c), lambda i: (layer, i, 0))
        in_specs += [layer_tile(hb, pool_width), layer_tile(bb, n_state_cols),
                     layer_tile(bb, n_state_cols)]
        args = [x, p] + list(state)
        state_spec = lambda r, c: pl.BlockSpec((r, c), lambda i: (i, 0))
        x_out_spec = pl.BlockSpec((m, d_model), lambda i: (i, 0))
    late = (w_out, w_gu, w_down, w_ple, w_ple_gate)
    in_specs += [hbm if any(w is v for v in late) else resident(w) for w in weights]
    in_specs.append(pl.BlockSpec(g_final.shape, lambda i: (0, 0), pipeline_mode=pl.Buffered(1)))
    args += list(weights) + [g_final]

    n_state_rows = bb if carry else n_tiles * bb
    n_hist_rows = hb if carry else n_tiles * hb
    out_shape = [jax.ShapeDtypeStruct(x.shape, jnp.float32),
                 jax.ShapeDtypeStruct((n_hist_rows, pool_width), jnp.float32),
                 jax.ShapeDtypeStruct((n_state_rows, n_state_cols), jnp.float32),
                 jax.ShapeDtypeStruct((n_state_rows, n_state_cols), jnp.float32)]
    out_specs = [x_out_spec, state_spec(hb, pool_width), state_spec(bb, n_state_cols),
                 state_spec(bb, n_state_cols)]
    scratch = [pltpu.VMEM((hb + m, pool_width), jnp.float32),
               pltpu.VMEM((m, 2 * n_state_cols), jnp.float32),
               pltpu.VMEM((2 if carry else 1, m, w_out.shape[1]), jnp.bfloat16),
               pltpu.VMEM((m, d_ff), jnp.bfloat16),
               pltpu.VMEM((2, SUBLANE, n_state_cols), jnp.float32)]
    if carry:
        scratch += [pltpu.VMEM((3, tc, bb, d_model), jnp.float32),
                    pltpu.VMEM((2, tc, bb, ple_dim), jnp.float32),
                    pltpu.VMEM((2, tc, bb, d_model), jnp.float32),
                    pltpu.SemaphoreType.DMA((3,)),
                    pltpu.SemaphoreType.DMA((2,)),
                    pltpu.SemaphoreType.DMA((2,))]
    scratch += [pltpu.VMEM(v.shape[1:], v.dtype) for v in late]
    scratch.append(pltpu.SemaphoreType.DMA((len(late),)))
    kern = functools.partial(_layer_kernel, layer=layer, tc=tc, bb=bb, carry=carry, pos0=pos0,
                             final_norm=final_norm, n_ssm_blocks=n_ssm_blocks, d_ff=d_ff)
    return pl.pallas_call(
        kern,
        grid=(n_tiles + 1 if carry else n_tiles,),
        in_specs=in_specs,
        out_specs=out_specs,
        out_shape=out_shape,
        scratch_shapes=scratch,
        compiler_params=pltpu.CompilerParams(dimension_semantics=("arbitrary",),
                                             vmem_limit_bytes=VMEM_LIMIT_BYTES),
        name=name,
    )(*args)


PROMPT_TC = 64
SAMPLE_BB = 64


def kernel(x_prompt, x_sample, state_pool, state_ssm_re, state_ssm_im, p_prompt, p_sample, g_mix, w_in, w_pool, pool_scale, ssm_a_re, ssm_a_im, ssm_log_dt, ssm_b_re, ssm_b_im, ssm_c_re, ssm_c_im, ssm_d, w_glu, b_glu, w_out, g_ffn, w_gate_up, w_down, g_ple, w_ple, w_ple_gate, g_final):
    depth = w_in.shape[0]
    batch, seq, d_model = x_prompt.shape
    dec_batch, dec_seq, _ = x_sample.shape
    n_groups, n_state = ssm_a_re.shape[1:]
    n_state_cols = n_groups * n_state
    pool_width = pool_scale.shape[-1]
    n_ssm_blocks = n_groups // SSM_BLOCK_GROUPS
    bf16 = jnp.bfloat16
    assert batch == SUBLANE and seq % PROMPT_TC == 0 and dec_batch % SAMPLE_BB == 0

    lam_re, lam_im, bb_re, bb_im = _ssm_prep(ssm_a_re, ssm_a_im, ssm_log_dt, ssm_b_re, ssm_b_im)

    def b_blocks(bb):
        w = jnp.transpose(bb, (0, 2, 1, 3)).reshape(
            depth, n_ssm_blocks, SSM_BLOCK_GROUPS * SSM_GROUP_DIM, n_state)
        return _block_diag(w, SSM_BLOCK_GROUPS)

    def c_blocks(c):
        w = jnp.transpose(c, (0, 1, 3, 2)).reshape(
            depth, n_ssm_blocks, SSM_BLOCK_GROUPS * n_state, SSM_GROUP_DIM)
        return _block_diag(w, SSM_BLOCK_GROUPS)

    b_cat = jnp.concatenate([b_blocks(bb_re), b_blocks(bb_im)], axis=-1).astype(bf16)
    c_cat = jnp.concatenate([c_blocks(ssm_c_re), -c_blocks(ssm_c_im)], axis=-2).astype(bf16)
    row = lambda a: a[:, None, :]
    weights = (row(g_mix), w_in.astype(bf16), w_pool.astype(bf16), row(pool_scale),
               lam_re.reshape(depth, 1, n_state_cols), lam_im.reshape(depth, 1, n_state_cols),
               b_cat, c_cat, row(ssm_d), w_glu.astype(bf16), row(b_glu), w_out.astype(bf16),
               row(g_ffn), w_gate_up.astype(bf16), w_down.astype(bf16), row(g_ple),
               w_ple.astype(bf16), w_ple_gate.astype(bf16))
    g_fin = g_final[None, :]

    xp = x_prompt
    hist_p, re_p, im_p = [], [], []
    for l in range(depth):
        xp, hist, h_re, h_im = _layer_call(
            l, xp, p_prompt, None, weights, g_fin, tc=PROMPT_TC, bb=batch, carry=True, pos0=0,
            final_norm=(l == depth - 1), name=f"prompt_layer{l}")
        hist_p.append(hist)
        re_p.append(h_re)
        im_p.append(h_im)
    pool_p = jnp.transpose(jnp.stack(hist_p).reshape(depth, POOL_BUF, batch, pool_width),
                           (0, 2, 1, 3))
    re_p = jnp.stack(re_p).reshape(depth, batch, n_groups, n_state)
    im_p = jnp.stack(im_p).reshape(depth, batch, n_groups, n_state)

    bb = SAMPLE_BB
    nt = dec_batch // bb

    def to_tiles(a):
        lead, (t, w) = a.shape[:-3], a.shape[-2:]
        a = jnp.swapaxes(a.reshape(lead + (nt, bb, t, w)), -3, -2)
        return a.reshape(lead + (nt * t * bb, w))

    def from_tiles(a, t):
        lead, w = a.shape[:-2], a.shape[-1]
        a = jnp.swapaxes(a.reshape(lead + (nt, t, bb, w)), -3, -2)
        return a.reshape(lead + (dec_batch, t, w))

    xs = to_tiles(x_sample)
    state = (to_tiles(state_pool),
             state_ssm_re.reshape(depth, dec_batch, n_state_cols),
             state_ssm_im.reshape(depth, dec_batch, n_state_cols))
    ps = to_tiles(p_sample.astype(bf16))
    hist_s, re_s, im_s = [], [], []
    for l in range(depth):
        xs, hist, h_re, h_im = _layer_call(
            l, xs, ps, state, weights, g_fin, tc=dec_seq, bb=bb, carry=False, pos0=PAST_LEN,
            final_norm=(l == depth - 1), name=f"sample_layer{l}")
        hist_s.append(hist)
        re_s.append(h_re)
        im_s.append(h_im)
    pool_s = from_tiles(jnp.stack(hist_s), POOL_BUF)
    re_s = jnp.stack(re_s).reshape(depth, dec_batch, n_groups, n_state)
    im_s = jnp.stack(im_s).reshape(depth, dec_batch, n_groups, n_state)

    return (xp, from_tiles(xs, dec_seq), pool_p, re_p, im_p, pool_s, re_s, im_s)
```

```python
import functools

import numpy as np
import jax
import jax.numpy as jnp
from jax import lax
from jax.experimental import pallas as pl
from jax.experimental.pallas import tpu as pltpu

LANE = 128
SUBLANE = 8
VMEM_LIMIT_BYTES = 60 * 1024 * 1024

POOL_WINDOWS = (2, 4, 8, 16)
POOL_BUF = max(POOL_WINDOWS) - 1
SSM_GROUP_DIM = 16
SSM_STATE = 64
EPS = 1e-6
PAST_LEN = 16384

SSM_BLOCK_CH = LANE
SSM_BLOCK_GROUPS = SSM_BLOCK_CH // SSM_GROUP_DIM
SSM_BLOCK_STATES = SSM_BLOCK_GROUPS * SSM_STATE

FFN_CHUNK = 256
POOL_PAIR = 2

_SQRT_HALF = np.float32(np.sqrt(0.5))


def _rms_scale(x):
    return lax.rsqrt(jnp.mean(x * x, axis=-1, keepdims=True) + EPS)


def _rms(x, g):
    return x * _rms_scale(x) * g


def _rms_dot(x, g, w):
    return _bdot(x * g, w) * _rms_scale(x)


def _bdot(a, b):
    return jnp.dot(a.astype(jnp.bfloat16), b, preferred_element_type=jnp.float32)


def _ssm_prep_kernel(a_re_ref, a_im_ref, log_dt_ref, b_re_ref, b_im_ref,
                     lam_re_ref, lam_im_ref, bb_re_ref, bb_im_ref):
    ar = a_re_ref[...]
    ai = a_im_ref[...]
    dt = jnp.exp(log_dt_ref[...])
    mag = jnp.exp(dt * ar)
    lam_re = mag * jnp.cos(dt * ai)
    lam_im = mag * jnp.sin(dt * ai)
    den = ar * ar + ai * ai
    nr = lam_re - 1.0
    k_re = (nr * ar + lam_im * ai) / den
    k_im = (lam_im * ar - nr * ai) / den
    lam_re_ref[...] = lam_re
    lam_im_ref[...] = lam_im
    for c in range(SSM_GROUP_DIM):
        br = b_re_ref[c]
        bi = b_im_ref[c]
        bb_re_ref[c] = k_re * br - k_im * bi
        bb_im_ref[c] = k_re * bi + k_im * br


def _ssm_prep(a_re, a_im, log_dt, b_re, b_im):
    depth, n_groups, n_state = a_re.shape
    b_re_t = jnp.transpose(b_re, (0, 3, 1, 2))
    b_im_t = jnp.transpose(b_im, (0, 3, 1, 2))
    gp = pl.BlockSpec((None, n_groups, n_state), lambda l: (l, 0, 0))
    cgp = pl.BlockSpec((None, SSM_GROUP_DIM, n_groups, n_state), lambda l: (l, 0, 0, 0))
    return pl.pallas_call(
        _ssm_prep_kernel,
        grid=(depth,),
        in_specs=[gp, gp, pl.BlockSpec((None, n_groups, 1), lambda l: (l, 0, 0)), cgp, cgp],
        out_specs=[gp, gp, cgp, cgp],
        out_shape=[jax.ShapeDtypeStruct(a_re.shape, jnp.float32)] * 2
        + [jax.ShapeDtypeStruct(b_re_t.shape, jnp.float32)] * 2,
        name="ssm_prep",
    )(a_re, a_im, log_dt[..., None], b_re_t, b_im_t)


def _block_diag(w, gb):
    rows, c = w.shape[-2:]
    r = rows // gb
    same_block = (jnp.arange(rows)[:, None] // r) == (jnp.arange(gb * c)[None, :] // c)
    return jnp.where(same_block, jnp.tile(w, (1, 1, 1, gb)), 0.0)


def _start_all(copies):
    for c in copies:
        c.start()


def _wait_all(copies):
    for c in copies:
        c.wait()


def _layer_kernel(*refs, layer, tc, bb, carry, pos0, final_norm, n_ssm_blocks, d_ff):
    it = iter(refs)
    x_ref = next(it)
    p_ref = next(it)
    if not carry:
        hist_in_ref = next(it)
        h_re_in_ref = next(it)
        h_im_in_ref = next(it)
    (g_mix_ref, w_in_ref, w_pool_ref, pool_scale_ref, lam_re_ref, lam_im_ref,
     b_ref, c_ref, d_skip_ref, w_glu_ref, b_glu_ref, w_out_ref, g_ffn_ref,
     w_gu_ref, w_down_ref, g_ple_ref, w_ple_ref, w_ple_gate_ref, g_final_ref) = (
         next(it) for _ in range(19))
    o_ref = next(it)
    hist_out_ref = next(it)
    h_re_ref = next(it)
    h_im_ref = next(it)
    ue_ref = next(it)
    bu_ref = next(it)
    mix_ref = next(it)
    act_ref = next(it)
    lam_b_ref = next(it)
    if carry:
        xbuf_ref = next(it)
        pbuf_ref = next(it)
        obuf_ref = next(it)
        x_sem = next(it)
        p_sem = next(it)
        o_sem = next(it)
    late_hbm = (w_out_ref, w_gu_ref, w_down_ref, w_ple_ref, w_ple_gate_ref)
    late_buf = tuple(next(it) for _ in late_hbm)
    w_sem = next(it)
    w_out_ref, w_gu_ref, w_down_ref, w_ple_ref, w_ple_gate_ref = late_buf

    def late_copies():
        return [pltpu.make_async_copy(h.at[layer], v, w_sem.at[i])
                for i, (h, v) in enumerate(zip(late_hbm, late_buf))]

    m = tc * bb
    hb = POOL_BUF * bb
    pool_width = ue_ref.shape[1]
    bs = SSM_BLOCK_STATES
    n_cols = n_ssm_blocks * bs
    pid = pl.program_id(0)

    def stage_a(x, a_slot, tile):
        z = _rms_dot(x, g_mix_ref[...], w_in_ref[...])
        ue_ref[hb:hb + m, :] = z[:, :pool_width]
        u_ssm = z[:, pool_width:]

        t_local = lax.shift_right_logical(
            lax.broadcasted_iota(jnp.int32, (m, LANE), 0), int(np.log2(bb)))
        pos = t_local + (pos0 + tile * tc)
        diffs = []
        for g, w in enumerate(POOL_WINDOWS):
            cols = slice(g * LANE, (g + 1) * LANE)
            cur = ue_ref[hb:hb + m, cols]
            s = cur
            for j in range(1, w):
                s = s + ue_ref[hb - j * bb:hb - j * bb + m, cols]
            cnt = jnp.minimum(pos + 1, w).astype(jnp.float32)
            diffs.append((s / cnt - cur).astype(jnp.bfloat16))
        for q in range(len(POOL_WINDOWS) // POOL_PAIR):
            cols = slice(q * POOL_PAIR * LANE, (q + 1) * POOL_PAIR * LANE)
            d = jnp.concatenate(diffs[q * POOL_PAIR:(q + 1) * POOL_PAIR], axis=-1)
            y = jnp.dot(d, w_pool_ref[q], preferred_element_type=jnp.float32)
            mix_ref[a_slot, :, cols] = (y * pool_scale_ref[:, cols]).astype(jnp.bfloat16)
        hist_out_ref[...] = ue_ref[m:m + hb, :]
        if carry:
            ue_ref[0:hb, :] = ue_ref[m:m + hb, :]

        for i in range(n_ssm_blocks):
            u_i = u_ssm[:, i * SSM_BLOCK_CH:(i + 1) * SSM_BLOCK_CH]
            bu_ref[:, 2 * bs * i:2 * bs * (i + 1)] = _bdot(u_i, b_ref[i])
        lam_b_ref[0] = jnp.broadcast_to(lam_re_ref[...], (SUBLANE, n_cols))
        lam_b_ref[1] = jnp.broadcast_to(lam_im_ref[...], (SUBLANE, n_cols))
        for rg in range(bb // SUBLANE):
            r_off = rg * SUBLANE

            def step(t, h, r_off=r_off):
                rows = pl.ds(pl.multiple_of(t * bb + r_off, SUBLANE), SUBLANE)
                new = []
                for i in range(n_ssm_blocks):
                    c_re, c_im, s0 = 2 * bs * i, 2 * bs * i + bs, bs * i
                    h_re, h_im = h[2 * i], h[2 * i + 1]
                    lre = lam_b_ref[0, :, s0:s0 + bs]
                    lim = lam_b_ref[1, :, s0:s0 + bs]
                    n_re = lre * h_re - lim * h_im + bu_ref[rows, c_re:c_re + bs]
                    n_im = lre * h_im + lim * h_re + bu_ref[rows, c_im:c_im + bs]
                    bu_ref[rows, c_re:c_re + bs] = n_re
                    bu_ref[rows, c_im:c_im + bs] = n_im
                    new += [n_re, n_im]
                return tuple(new)

            h0 = []
            for i in range(n_ssm_blocks):
                h0 += [h_re_ref[r_off:r_off + SUBLANE, bs * i:bs * (i + 1)],
                       h_im_ref[r_off:r_off + SUBLANE, bs * i:bs * (i + 1)]]
            h = lax.fori_loop(0, tc, step, tuple(h0), unroll=True if carry else min(tc, 8))
            for i in range(n_ssm_blocks):
                h_re_ref[r_off:r_off + SUBLANE, bs * i:bs * (i + 1)] = h[2 * i]
                h_im_ref[r_off:r_off + SUBLANE, bs * i:bs * (i + 1)] = h[2 * i + 1]
        ys = []
        for i in range(n_ssm_blocks):
            cols = slice(i * SSM_BLOCK_CH, (i + 1) * SSM_BLOCK_CH)
            ys.append(_bdot(bu_ref[:, 2 * bs * i:2 * bs * (i + 1)], c_ref[i])
                      + d_skip_ref[:, cols] * u_ssm[:, cols])
        y = jnp.concatenate(ys, axis=-1)
        gl = 0.5 * y * (1.0 + lax.erf(y * _SQRT_HALF))
        y_ssm = gl * jax.nn.sigmoid(_bdot(gl, w_glu_ref[...]) + b_glu_ref[...])
        mix_ref[a_slot, :, pool_width:] = y_ssm.astype(jnp.bfloat16)

    def stage_b(x, p, b_slot):
        x = x + jnp.dot(mix_ref[b_slot], w_out_ref[...], preferred_element_type=jnp.float32)
        hn = (x * g_ffn_ref[...]).astype(jnp.bfloat16)
        hs = _rms_scale(x)
        for j in range(d_ff // FFN_CHUNK):
            c0 = j * FFN_CHUNK
            gate = hs * jnp.dot(hn, w_gu_ref[:, c0:c0 + FFN_CHUNK],
                                preferred_element_type=jnp.float32)
            up = hs * jnp.dot(hn, w_gu_ref[:, d_ff + c0:d_ff + c0 + FFN_CHUNK],
                              preferred_element_type=jnp.float32)
            act_ref[:, c0:c0 + FFN_CHUNK] = (
                gate * jax.nn.sigmoid(gate) * up).astype(jnp.bfloat16)
        x = x + jnp.dot(act_ref[...], w_down_ref[...], preferred_element_type=jnp.float32)
        gate = jax.nn.sigmoid(_rms_dot(x, g_ple_ref[...], w_ple_gate_ref[...]))
        x = x + _bdot(p, w_ple_ref[...]) * gate
        if final_norm:
            x = _rms(x, g_final_ref[...])
        return x

    if not carry:
        @pl.when(pid == 0)
        def _():
            _start_all(late_copies())

        ue_ref[0:hb, :] = hist_in_ref[...]
        h_re_ref[...] = h_re_in_ref[...]
        h_im_ref[...] = h_im_in_ref[...]
        x = x_ref[...]
        stage_a(x, 0, 0)

        @pl.when(pid == 0)
        def _():
            _wait_all(late_copies())

        o_ref[...] = stage_b(x, p_ref[...], 0)
        return

    n = pl.num_programs(0) - 1

    def x_copies(tile):
        s = lax.rem(tile, 3)
        return [pltpu.make_async_copy(x_ref.at[b, pl.ds(tile * tc, tc), :],
                                      xbuf_ref.at[s, :, b, :], x_sem.at[s]) for b in range(bb)]

    def p_copies(tile):
        s = lax.rem(tile, 2)
        return [pltpu.make_async_copy(p_ref.at[layer, b, pl.ds(tile * tc, tc), :],
                                      pbuf_ref.at[s, :, b, :], p_sem.at[s]) for b in range(bb)]

    def o_copies(tile):
        s = lax.rem(tile, 2)
        return [pltpu.make_async_copy(obuf_ref.at[s, :, b, :],
                                      o_ref.at[b, pl.ds(tile * tc, tc), :], o_sem.at[s])
                for b in range(bb)]

    def x_tile(tile):
        return xbuf_ref[lax.rem(tile, 3)].reshape(m, xbuf_ref.shape[-1])

    def run_b(tile):
        _wait_all(p_copies(tile))
        out = stage_b(x_tile(tile), pbuf_ref[lax.rem(tile, 2)].reshape(m, pbuf_ref.shape[-1]),
                      lax.rem(tile, 2))
        obuf_ref[lax.rem(tile, 2)] = out.reshape(tc, bb, out.shape[-1])
        _start_all(o_copies(tile))

    @pl.when(pid == 0)
    def _():
        _start_all(x_copies(0) + late_copies())
        ue_ref[0:hb, :] = jnp.zeros((hb, pool_width), jnp.float32)
        h_re_ref[...] = jnp.zeros(h_re_ref.shape, jnp.float32)
        h_im_ref[...] = jnp.zeros(h_im_ref.shape, jnp.float32)

    @pl.when(pid + 1 < n)
    def _():
        _start_all(x_copies(pid + 1))

    @pl.when(pid < n)
    def _():
        _start_all(p_copies(pid))

    @pl.when(pid >= 3)
    def _():
        _wait_all(o_copies(pid - 3))

    @pl.when(pid == 1)
    def _():
        _wait_all(late_copies())

    @pl.when(pid == 0)
    def _():
        _wait_all(x_copies(0))
        stage_a(x_tile(0), 0, 0)

    @pl.when((pid > 0) & (pid < n))
    def _():
        _wait_all(x_copies(pid))
        stage_a(x_tile(pid), lax.rem(pid, 2), pid)
        run_b(pid - 1)

    @pl.when(pid == n)
    def _():
        run_b(n - 1)

        @pl.when(n >= 2)
        def _():
            _wait_all(o_copies(n - 2))

        _wait_all(o_copies(n - 1))


def _layer_call(layer, x, p, state, weights, g_final, *, tc, bb, carry, pos0, final_norm, name):
    m = tc * bb
    hb = POOL_BUF * bb
    d_model = x.shape[-1]
    ple_dim = p.shape[-1]
    n_tiles = (x.shape[1] // tc) if carry else (x.shape[0] // m)
    (g_mix, w_in, w_pool, pool_scale, lam_re, lam_im, b_cat, c_cat, d_skip, w_glu, b_glu,
     w_out, g_ffn, w_gu, w_down, g_ple, w_ple, w_ple_gate) = weights
    pool_width = pool_scale.shape[-1]
    n_ssm_blocks = b_cat.shape[1]
    n_state_cols = lam_re.shape[-1]
    d_ff = w_down.shape[1]

    def resident(arr):
        nd = arr.ndim - 1
        return pl.BlockSpec((None,) + arr.shape[1:], lambda i, nd=nd: (layer,) + (0,) * nd,
                            pipeline_mode=pl.Buffered(1))

    hbm = pl.BlockSpec(memory_space=pl.ANY)
    if carry:
        in_specs = [hbm, hbm]
        args = [x, p]
        state_spec = lambda r, c: pl.BlockSpec((r, c), lambda i: (0, 0))
        x_out_spec = hbm
    else:
        in_specs = [pl.BlockSpec((m, d_model), lambda i: (i, 0)),
                    pl.BlockSpec((None, m, ple_dim), lambda i: (layer, i, 0))]
        layer_tile = lambda r, c: pl.BlockSpec((None, r, c), lambda i: (layer, i, 0))
        in_specs += [layer_tile(hb, pool_width), layer_tile(bb, n_state_cols),
                     layer_tile(bb, n_state_cols)]
        args = [x, p] + list(state)
        state_spec = lambda r, c: pl.BlockSpec((r, c), lambda i: (i, 0))
        x_out_spec = pl.BlockSpec((m, d_model), lambda i: (i, 0))
    late = (w_out, w_gu, w_down, w_ple, w_ple_gate)
    in_specs += [hbm if any(w is v for v in late) else resident(w) for w in weights]
    in_specs.append(pl.BlockSpec(g_final.shape, lambda i: (0, 0), pipeline_mode=pl.Buffered(1)))
    args += list(weights) + [g_final]

    n_state_rows = bb if carry else n_tiles * bb
    n_hist_rows = hb if carry else n_tiles * hb
    out_shape = [jax.ShapeDtypeStruct(x.shape, jnp.float32),
                 jax.ShapeDtypeStruct((n_hist_rows, pool_width), jnp.float32),
                 jax.ShapeDtypeStruct((n_state_rows, n_state_cols), jnp.float32),
                 jax.ShapeDtypeStruct((n_state_rows, n_state_cols), jnp.float32)]
    out_specs = [x_out_spec, state_spec(hb, pool_width), state_spec(bb, n_state_cols),
                 state_spec(bb, n_state_cols)]
    scratch = [pltpu.VMEM((hb + m, pool_width), jnp.float32),
               pltpu.VMEM((m, 2 * n_state_cols), jnp.float32),
               pltpu.VMEM((2 if carry else 1, m, w_out.shape[1]), jnp.bfloat16),
               pltpu.VMEM((m, d_ff), jnp.bfloat16),
               pltpu.VMEM((2, SUBLANE, n_state_cols), jnp.float32)]
    if carry:
        scratch += [pltpu.VMEM((3, tc, bb, d_model), jnp.float32),
                    pltpu.VMEM((2, tc, bb, ple_dim), jnp.float32),
                    pltpu.VMEM((2, tc, bb, d_model), jnp.float32),
                    pltpu.SemaphoreType.DMA((3,)),
                    pltpu.SemaphoreType.DMA((2,)),
                    pltpu.SemaphoreType.DMA((2,))]
    scratch += [pltpu.VMEM(v.shape[1:], v.dtype) for v in late]
    scratch.append(pltpu.SemaphoreType.DMA((len(late),)))
    kern = functools.partial(_layer_kernel, layer=layer, tc=tc, bb=bb, carry=carry, pos0=pos0,
                             final_norm=final_norm, n_ssm_blocks=n_ssm_blocks, d_ff=d_ff)
    return pl.pallas_call(
        kern,
        grid=(n_tiles + 1 if carry else n_tiles,),
        in_specs=in_specs,
        out_specs=out_specs,
        out_shape=out_shape,
        scratch_shapes=scratch,
        compiler_params=pltpu.CompilerParams(dimension_semantics=("arbitrary",),
                                             vmem_limit_bytes=VMEM_LIMIT_BYTES),
        name=name,
    )(*args)


PROMPT_TC = 64
SAMPLE_BB = 64


def kernel(x_prompt, x_sample, state_pool, state_ssm_re, state_ssm_im, p_prompt, p_sample, g_mix, w_in, w_pool, pool_scale, ssm_a_re, ssm_a_im, ssm_log_dt, ssm_b_re, ssm_b_im, ssm_c_re, ssm_c_im, ssm_d, w_glu, b_glu, w_out, g_ffn, w_gate_up, w_down, g_ple, w_ple, w_ple_gate, g_final):
    depth = w_in.shape[0]
    batch, seq, d_model = x_prompt.shape
    dec_batch, dec_seq, _ = x_sample.shape
    n_groups, n_state = ssm_a_re.shape[1:]
    n_state_cols = n_groups * n_state
    pool_width = pool_scale.shape[-1]
    n_ssm_blocks = n_groups // SSM_BLOCK_GROUPS
    bf16 = jnp.bfloat16
    assert batch == SUBLANE and seq % PROMPT_TC == 0 and dec_batch % SAMPLE_BB == 0

    lam_re, lam_im, bb_re, bb_im = _ssm_prep(ssm_a_re, ssm_a_im, ssm_log_dt, ssm_b_re, ssm_b_im)

    def b_blocks(bb):
        w = jnp.transpose(bb, (0, 2, 1, 3)).reshape(
            depth, n_ssm_blocks, SSM_BLOCK_GROUPS * SSM_GROUP_DIM, n_state)
        return _block_diag(w, SSM_BLOCK_GROUPS)

    def c_blocks(c):
        w = jnp.transpose(c, (0, 1, 3, 2)).reshape(
            depth, n_ssm_blocks, SSM_BLOCK_GROUPS * n_state, SSM_GROUP_DIM)
        return _block_diag(w, SSM_BLOCK_GROUPS)

    b_cat = jnp.concatenate([b_blocks(bb_re), b_blocks(bb_im)], axis=-1).astype(bf16)
    c_cat = jnp.concatenate([c_blocks(ssm_c_re), -c_blocks(ssm_c_im)], axis=-2).astype(bf16)
    row = lambda a: a[:, None, :]
    n_pool, pool_dim = w_pool.shape[1:3]
    w_pool_pairs = _block_diag(
        w_pool.reshape(depth, n_pool // POOL_PAIR, POOL_PAIR * pool_dim, pool_dim), POOL_PAIR)
    weights = (row(g_mix), w_in.astype(bf16), w_pool_pairs.astype(bf16), row(pool_scale),
               lam_re.reshape(depth, 1, n_state_cols), lam_im.reshape(depth, 1, n_state_cols),
               b_cat, c_cat, row(ssm_d), w_glu.astype(bf16), row(b_glu), w_out.astype(bf16),
               row(g_ffn), w_gate_up.astype(bf16), w_down.astype(bf16), row(g_ple),
               w_ple.astype(bf16), w_ple_gate.astype(bf16))
    g_fin = g_final[None, :]

    xp = x_prompt
    hist_p, re_p, im_p = [], [], []
    for l in range(depth):
        xp, hist, h_re, h_im = _layer_call(
            l, xp, p_prompt, None, weights, g_fin, tc=PROMPT_TC, bb=batch, carry=True, pos0=0,
            final_norm=(l == depth - 1), name=f"prompt_layer{l}")
        hist_p.append(hist)
        re_p.append(h_re)
        im_p.append(h_im)
    pool_p = jnp.transpose(jnp.stack(hist_p).reshape(depth, POOL_BUF, batch, pool_width),
                           (0, 2, 1, 3))
    re_p = jnp.stack(re_p).reshape(depth, batch, n_groups, n_state)
    im_p = jnp.stack(im_p).reshape(depth, batch, n_groups, n_state)

    bb = SAMPLE_BB
    nt = dec_batch // bb

    def to_tiles(a):
        lead, (t, w) = a.shape[:-3], a.shape[-2:]
        a = jnp.swapaxes(a.reshape(lead + (nt, bb, t, w)), -3, -2)
        return a.reshape(lead + (nt * t * bb, w))

    def from_tiles(a, t):
        lead, w = a.shape[:-2], a.shape[-1]
        a = jnp.swapaxes(a.reshape(lead + (nt, t, bb, w)), -3, -2)
        return a.reshape(lead + (dec_batch, t, w))

    xs = to_tiles(x_sample)
    state = (to_tiles(state_pool),
             state_ssm_re.reshape(depth, dec_batch, n_state_cols),
             state_ssm_im.reshape(depth, dec_batch, n_state_cols))
    ps = to_tiles(p_sample.astype(bf16))
    hist_s, re_s, im_s = [], [], []
    for l in range(depth):
        xs, hist, h_re, h_im = _layer_call(
            l, xs, ps, state, weights, g_fin, tc=dec_seq, bb=bb, carry=False, pos0=PAST_LEN,
            final_norm=(l == depth - 1), name=f"sample_layer{l}")
        hist_s.append(hist)
        re_s.append(h_re)
        im_s.append(h_im)
    pool_s = from_tiles(jnp.stack(hist_s), POOL_BUF)
    re_s = jnp.stack(re_s).reshape(depth, dec_batch, n_groups, n_state)
    im_s = jnp.stack(im_s).reshape(depth, dec_batch, n_groups, n_state)

    return (xp, from_tiles(xs, dec_seq), pool_p, re_p, im_p, pool_s, re_s, im_s)
```

```python
import functools

import numpy as np
import jax
import jax.numpy as jnp
from jax import lax
from jax.experimental import pallas as pl
from jax.experimental.pallas import tpu as pltpu

LANE = 128
SUBLANE = 8
VMEM_LIMIT_BYTES = 60 * 1024 * 1024

POOL_WINDOWS = (2, 4, 8, 16)
POOL_BUF = max(POOL_WINDOWS) - 1
SSM_GROUP_DIM = 16
SSM_STATE = 64
EPS = 1e-6
PAST_LEN = 16384

SSM_BLOCK_CH = LANE
SSM_BLOCK_GROUPS = SSM_BLOCK_CH // SSM_GROUP_DIM
SSM_BLOCK_STATES = SSM_BLOCK_GROUPS * SSM_STATE

FFN_CHUNK = 256
POOL_PAIR = 2

_SQRT_HALF = np.float32(np.sqrt(0.5))


def _rms_scale(x):
    return lax.rsqrt(jnp.mean(x * x, axis=-1, keepdims=True) + EPS)


def _rms(x, g):
    return x * _rms_scale(x) * g


def _rms_dot(x, g, w):
    return _bdot(x * g, w) * _rms_scale(x)


def _bdot(a, b):
    return jnp.dot(a.astype(jnp.bfloat16), b, preferred_element_type=jnp.float32)


def _ssm_prep_kernel(a_re_ref, a_im_ref, log_dt_ref, b_re_ref, b_im_ref,
                     lam_re_ref, lam_im_ref, bb_re_ref, bb_im_ref):
    ar = a_re_ref[...]
    ai = a_im_ref[...]
    dt = jnp.exp(log_dt_ref[...])
    mag = jnp.exp(dt * ar)
    lam_re = mag * jnp.cos(dt * ai)
    lam_im = mag * jnp.sin(dt * ai)
    den = ar * ar + ai * ai
    nr = lam_re - 1.0
    k_re = (nr * ar + lam_im * ai) / den
    k_im = (lam_im * ar - nr * ai) / den
    lam_re_ref[...] = lam_re
    lam_im_ref[...] = lam_im
    for c in range(SSM_GROUP_DIM):
        br = b_re_ref[c]
        bi = b_im_ref[c]
        bb_re_ref[c] = k_re * br - k_im * bi
        bb_im_ref[c] = k_re * bi + k_im * br


def _ssm_prep(a_re, a_im, log_dt, b_re, b_im):
    depth, n_groups, n_state = a_re.shape
    b_re_t = jnp.transpose(b_re, (0, 3, 1, 2))
    b_im_t = jnp.transpose(b_im, (0, 3, 1, 2))
    gp = pl.BlockSpec((None, n_groups, n_state), lambda l: (l, 0, 0))
    cgp = pl.BlockSpec((None, SSM_GROUP_DIM, n_groups, n_state), lambda l: (l, 0, 0, 0))
    return pl.pallas_call(
        _ssm_prep_kernel,
        grid=(depth,),
        in_specs=[gp, gp, pl.BlockSpec((None, n_groups, 1), lambda l: (l, 0, 0)), cgp, cgp],
        out_specs=[gp, gp, cgp, cgp],
        out_shape=[jax.ShapeDtypeStruct(a_re.shape, jnp.float32)] * 2
        + [jax.ShapeDtypeStruct(b_re_t.shape, jnp.float32)] * 2,
        name="ssm_prep",
    )(a_re, a_im, log_dt[..., None], b_re_t, b_im_t)


def _block_diag(w, gb):
    rows, c = w.shape[-2:]
    r = rows // gb
    same_block = (jnp.arange(rows)[:, None] // r) == (jnp.arange(gb * c)[None, :] // c)
    return jnp.where(same_block, jnp.tile(w, (1, 1, 1, gb)), 0.0)


def _start_all(copies):
    for c in copies:
        c.start()


def _wait_all(copies):
    for c in copies:
        c.wait()


def _layer_kernel(*refs, layer, tc, bb, carry, pos0, final_norm, n_ssm_blocks, d_ff,
                  cast_chunks):
    it = iter(refs)
    x_ref = next(it)
    p_ref = next(it)
    if not carry:
        hist_in_ref = next(it)
        h_re_in_ref = next(it)
        h_im_in_ref = next(it)
    (g_mix_ref, w_in_ref, w_pool_ref, pool_scale_ref, lam_re_ref, lam_im_ref,
     b_ref, c_ref, d_skip_ref, w_glu_ref, b_glu_ref, w_out_ref, g_ffn_ref,
     w_gu_ref, w_down_ref, g_ple_ref, w_ple_ref, w_ple_gate_ref, g_final_ref) = (
         next(it) for _ in range(19))
    cast_src = tuple(next(it) for _ in cast_chunks)
    o_ref = next(it)
    hist_out_ref = next(it)
    h_re_ref = next(it)
    h_im_ref = next(it)
    cast_dst = tuple(next(it) for _ in cast_chunks)
    ue_ref = next(it)
    bu_ref = next(it)
    mix_ref = next(it)
    act_ref = next(it)
    lam_b_ref = next(it)
    if carry:
        xbuf_ref = next(it)
        pbuf_ref = next(it)
        obuf_ref = next(it)
        x_sem = next(it)
        p_sem = next(it)
        o_sem = next(it)
    late_hbm = (w_out_ref, w_gu_ref, w_down_ref, w_ple_ref, w_ple_gate_ref)
    late_buf = tuple(next(it) for _ in late_hbm)
    w_sem = next(it)
    w_out_ref, w_gu_ref, w_down_ref, w_ple_ref, w_ple_gate_ref = late_buf

    def late_copies():
        return [pltpu.make_async_copy(h, v, w_sem.at[i])
                for i, (h, v) in enumerate(zip(late_hbm, late_buf))]

    cast_f32 = tuple(next(it) for _ in cast_chunks)
    cast_b16 = tuple(next(it) for _ in cast_chunks)
    if cast_chunks:
        cast_sem = next(it)

    m = tc * bb
    hb = POOL_BUF * bb
    pool_width = ue_ref.shape[1]
    bs = SSM_BLOCK_STATES
    n_cols = n_ssm_blocks * bs
    pid = pl.program_id(0)

    def stage_a(x, a_slot, tile):
        z = _rms_dot(x, g_mix_ref[...], w_in_ref[...])
        ue_ref[hb:hb + m, :] = z[:, :pool_width]
        u_ssm = z[:, pool_width:]

        t_local = lax.shift_right_logical(
            lax.broadcasted_iota(jnp.int32, (m, LANE), 0), int(np.log2(bb)))
        pos = t_local + (pos0 + tile * tc)
        diffs = []
        for g, w in enumerate(POOL_WINDOWS):
            cols = slice(g * LANE, (g + 1) * LANE)
            cur = ue_ref[hb:hb + m, cols]
            s = cur
            for j in range(1, w):
                s = s + ue_ref[hb - j * bb:hb - j * bb + m, cols]
            cnt = jnp.minimum(pos + 1, w).astype(jnp.float32)
            diffs.append((s / cnt - cur).astype(jnp.bfloat16))
        for q in range(len(POOL_WINDOWS) // POOL_PAIR):
            cols = slice(q * POOL_PAIR * LANE, (q + 1) * POOL_PAIR * LANE)
            d = jnp.concatenate(diffs[q * POOL_PAIR:(q + 1) * POOL_PAIR], axis=-1)
            y = jnp.dot(d, w_pool_ref[q], preferred_element_type=jnp.float32)
            mix_ref[a_slot, :, cols] = (y * pool_scale_ref[:, cols]).astype(jnp.bfloat16)
        hist_out_ref[...] = ue_ref[m:m + hb, :]
        if carry:
            ue_ref[0:hb, :] = ue_ref[m:m + hb, :]

        for i in range(n_ssm_blocks):
            u_i = u_ssm[:, i * SSM_BLOCK_CH:(i + 1) * SSM_BLOCK_CH]
            bu_ref[:, 2 * bs * i:2 * bs * (i + 1)] = _bdot(u_i, b_ref[i])
        lam_b_ref[0] = jnp.broadcast_to(lam_re_ref[...], (SUBLANE, n_cols))
        lam_b_ref[1] = jnp.broadcast_to(lam_im_ref[...], (SUBLANE, n_cols))
        for rg in range(bb // SUBLANE):
            r_off = rg * SUBLANE

            def step(t, h, r_off=r_off):
                rows = pl.ds(pl.multiple_of(t * bb + r_off, SUBLANE), SUBLANE)
                new = []
                for i in range(n_ssm_blocks):
                    c_re, c_im, s0 = 2 * bs * i, 2 * bs * i + bs, bs * i
                    h_re, h_im = h[2 * i], h[2 * i + 1]
                    lre = lam_b_ref[0, :, s0:s0 + bs]
                    lim = lam_b_ref[1, :, s0:s0 + bs]
                    n_re = lre * h_re - lim * h_im + bu_ref[rows, c_re:c_re + bs]
                    n_im = lre * h_im + lim * h_re + bu_ref[rows, c_im:c_im + bs]
                    bu_ref[rows, c_re:c_re + bs] = n_re
                    bu_ref[rows, c_im:c_im + bs] = n_im
                    new += [n_re, n_im]
                return tuple(new)

            h0 = []
            for i in range(n_ssm_blocks):
                h0 += [h_re_ref[r_off:r_off + SUBLANE, bs * i:bs * (i + 1)],
                       h_im_ref[r_off:r_off + SUBLANE, bs * i:bs * (i + 1)]]
            h = lax.fori_loop(0, tc, step, tuple(h0), unroll=True if carry else min(tc, 8))
            for i in range(n_ssm_blocks):
                h_re_ref[r_off:r_off + SUBLANE, bs * i:bs * (i + 1)] = h[2 * i]
                h_im_ref[r_off:r_off + SUBLANE, bs * i:bs * (i + 1)] = h[2 * i + 1]
        ys = []
        for i in range(n_ssm_blocks):
            cols = slice(i * SSM_BLOCK_CH, (i + 1) * SSM_BLOCK_CH)
            ys.append(_bdot(bu_ref[:, 2 * bs * i:2 * bs * (i + 1)], c_ref[i])
                      + d_skip_ref[:, cols] * u_ssm[:, cols])
        y = jnp.concatenate(ys, axis=-1)
        gl = 0.5 * y * (1.0 + lax.erf(y * _SQRT_HALF))
        y_ssm = gl * jax.nn.sigmoid(_bdot(gl, w_glu_ref[...]) + b_glu_ref[...])
        mix_ref[a_slot, :, pool_width:] = y_ssm.astype(jnp.bfloat16)

    def stage_b(x, p, b_slot):
        x = x + jnp.dot(mix_ref[b_slot], w_out_ref[...], preferred_element_type=jnp.float32)
        hn = (x * g_ffn_ref[...]).astype(jnp.bfloat16)
        hs = _rms_scale(x)
        for j in range(d_ff // FFN_CHUNK):
            c0 = j * FFN_CHUNK
            gate = hs * jnp.dot(hn, w_gu_ref[:, c0:c0 + FFN_CHUNK],
                                preferred_element_type=jnp.float32)
            up = hs * jnp.dot(hn, w_gu_ref[:, d_ff + c0:d_ff + c0 + FFN_CHUNK],
                              preferred_element_type=jnp.float32)
            act_ref[:, c0:c0 + FFN_CHUNK] = (
                gate * jax.nn.sigmoid(gate) * up).astype(jnp.bfloat16)
        x = x + jnp.dot(act_ref[...], w_down_ref[...], preferred_element_type=jnp.float32)
        gate = jax.nn.sigmoid(_rms_dot(x, g_ple_ref[...], w_ple_gate_ref[...]))
        x = x + _bdot(p, w_ple_ref[...]) * gate
        if final_norm:
            x = _rms(x, g_final_ref[...])
        return x

    if not carry:
        @pl.when(pid == 0)
        def _():
            _start_all(late_copies())

        ue_ref[0:hb, :] = hist_in_ref[...]
        h_re_ref[...] = h_re_in_ref[...]
        h_im_ref[...] = h_im_in_ref[...]
        x = x_ref[...]
        stage_a(x, 0, 0)

        @pl.when(pid == 0)
        def _():
            _wait_all(late_copies())

        o_ref[...] = stage_b(x, p_ref[...], 0)
        return

    n = pl.num_programs(0) - 1

    def x_copies(tile):
        s = lax.rem(tile, 3)
        return [pltpu.make_async_copy(x_ref.at[b, pl.ds(tile * tc, tc), :],
                                      xbuf_ref.at[s, :, b, :], x_sem.at[s]) for b in range(bb)]

    def p_copies(tile):
        s = lax.rem(tile, 2)
        return [pltpu.make_async_copy(p_ref.at[layer, b, pl.ds(tile * tc, tc), :],
                                      pbuf_ref.at[s, :, b, :], p_sem.at[s]) for b in range(bb)]

    def o_copies(tile):
        s = lax.rem(tile, 2)
        return [pltpu.make_async_copy(obuf_ref.at[s, :, b, :],
                                      o_ref.at[b, pl.ds(tile * tc, tc), :], o_sem.at[s])
                for b in range(bb)]

    def x_tile(tile):
        return xbuf_ref[lax.rem(tile, 3)].reshape(m, xbuf_ref.shape[-1])

    def run_b(tile):
        _wait_all(p_copies(tile))
        out = stage_b(x_tile(tile), pbuf_ref[lax.rem(tile, 2)].reshape(m, pbuf_ref.shape[-1]),
                      lax.rem(tile, 2))
        obuf_ref[lax.rem(tile, 2)] = out.reshape(tc, bb, out.shape[-1])
        _start_all(o_copies(tile))

    @pl.when(pid == 0)
    def _():
        _start_all(x_copies(0) + late_copies())
        ue_ref[0:hb, :] = jnp.zeros((hb, pool_width), jnp.float32)
        h_re_ref[...] = jnp.zeros(h_re_ref.shape, jnp.float32)
        h_im_ref[...] = jnp.zeros(h_im_ref.shape, jnp.float32)

    @pl.when(pid + 1 < n)
    def _():
        _start_all(x_copies(pid + 1))

    @pl.when(pid < n)
    def _():
        _start_all(p_copies(pid))

    @pl.when(pid >= 3)
    def _():
        _wait_all(o_copies(pid - 3))

    @pl.when(pid == 1)
    def _():
        _wait_all(late_copies())

    for i, (rows, n_chunks) in enumerate(cast_chunks):
        def c_in(chunk, i=i, rows=rows):
            return pltpu.make_async_copy(
                cast_src[i].at[layer + 1, pl.ds(chunk * rows, rows), :], cast_f32[i],
                cast_sem.at[0, i])

        def c_out(chunk, i=i, rows=rows):
            return pltpu.make_async_copy(
                cast_b16[i], cast_dst[i].at[pl.ds(chunk * rows, rows), :], cast_sem.at[1, i])

        @pl.when(pid == 0)
        def _():
            c_in(0).start()

        @pl.when(pid < n_chunks)
        def _():
            c_in(pid).wait()

            @pl.when(pid >= 1)
            def _():
                c_out(pid - 1).wait()

            cast_b16[i][...] = cast_f32[i][...].astype(jnp.bfloat16)
            c_out(pid).start()

            @pl.when(pid + 1 < n_chunks)
            def _():
                c_in(pid + 1).start()

        @pl.when(pid == n_chunks)
        def _():
            c_out(n_chunks - 1).wait()

    @pl.when(pid == 0)
    def _():
        _wait_all(x_copies(0))
        stage_a(x_tile(0), 0, 0)

    @pl.when((pid > 0) & (pid < n))
    def _():
        _wait_all(x_copies(pid))
        stage_a(x_tile(pid), lax.rem(pid, 2), pid)
        run_b(pid - 1)

    @pl.when(pid == n)
    def _():
        run_b(n - 1)

        @pl.when(n >= 2)
        def _():
            _wait_all(o_copies(n - 2))

        _wait_all(o_copies(n - 1))


def _cast_plan(rows, max_chunks):
    bf16_tile_rows = 2 * SUBLANE
    for r in range(bf16_tile_rows, rows + 1, bf16_tile_rows):
        if rows % r == 0 and rows // r <= max_chunks:
            return r, rows // r
    raise ValueError(f"no chunking of {rows} rows into <= {max_chunks} chunks")


def _layer_call(layer, x, p, state, weights, g_final, cast_src, *, tc, bb, carry, pos0,
                final_norm, name):
    m = tc * bb
    hb = POOL_BUF * bb
    d_model = x.shape[-1]
    ple_dim = p.shape[-1]
    n_tiles = (x.shape[1] // tc) if carry else (x.shape[0] // m)
    (g_mix, w_in, w_pool, pool_scale, lam_re, lam_im, b_cat, c_cat, d_skip, w_glu, b_glu,
     w_out, g_ffn, w_gu, w_down, g_ple, w_ple, w_ple_gate) = weights
    pool_width = pool_scale.shape[-1]
    n_ssm_blocks = b_cat.shape[1]
    n_state_cols = lam_re.shape[-1]
    d_ff = w_down.shape[0]
    late = (w_out, w_gu, w_down, w_ple, w_ple_gate)

    hbm = pl.BlockSpec(memory_space=pl.ANY)

    def weight_spec(arr):
        if any(arr is v for v in late):
            return hbm
        if arr is w_in:
            return pl.BlockSpec(arr.shape, lambda i: (0, 0), pipeline_mode=pl.Buffered(1))
        nd = arr.ndim - 1
        return pl.BlockSpec((None,) + arr.shape[1:], lambda i, nd=nd: (layer,) + (0,) * nd,
                            pipeline_mode=pl.Buffered(1))

    if carry:
        in_specs = [hbm, hbm]
        args = [x, p]
        state_spec = lambda r, c: pl.BlockSpec((r, c), lambda i: (0, 0))
        x_out_spec = hbm
    else:
        in_specs = [pl.BlockSpec((m, d_model), lambda i: (i, 0)),
                    pl.BlockSpec((None, m, ple_dim), lambda i: (layer, i, 0))]
        layer_tile = lambda r, c: pl.BlockSpec((None, r, c), lambda i: (layer, i, 0))
        in_specs += [layer_tile(hb, pool_width), layer_tile(bb, n_state_cols),
                     layer_tile(bb, n_state_cols)]
        args = [x, p] + list(state)
        state_spec = lambda r, c: pl.BlockSpec((r, c), lambda i: (i, 0))
        x_out_spec = pl.BlockSpec((m, d_model), lambda i: (i, 0))
    in_specs += [weight_spec(w) for w in weights]
    in_specs.append(pl.BlockSpec(g_final.shape, lambda i: (0, 0), pipeline_mode=pl.Buffered(1)))
    in_specs += [hbm] * len(cast_src)
    args += list(weights) + [g_final] + list(cast_src)

    cast_chunks = tuple(_cast_plan(a.shape[1], n_tiles) for a in cast_src)
    n_state_rows = bb if carry else n_tiles * bb
    n_hist_rows = hb if carry else n_tiles * hb
    out_shape = [jax.ShapeDtypeStruct(x.shape, jnp.float32),
                 jax.ShapeDtypeStruct((n_hist_rows, pool_width), jnp.float32),
                 jax.ShapeDtypeStruct((n_state_rows, n_state_cols), jnp.float32),
                 jax.ShapeDtypeStruct((n_state_rows, n_state_cols), jnp.float32)]
    out_shape += [jax.ShapeDtypeStruct(a.shape[1:], jnp.bfloat16) for a in cast_src]
    out_specs = [x_out_spec, state_spec(hb, pool_width), state_spec(bb, n_state_cols),
                 state_spec(bb, n_state_cols)] + [hbm] * len(cast_src)
    scratch = [pltpu.VMEM((hb + m, pool_width), jnp.float32),
               pltpu.VMEM((m, 2 * n_state_cols), jnp.float32),
               pltpu.VMEM((2 if carry else 1, m, w_out.shape[0]), jnp.bfloat16),
               pltpu.VMEM((m, d_ff), jnp.bfloat16),
               pltpu.VMEM((2, SUBLANE, n_state_cols), jnp.float32)]
    if carry:
        scratch += [pltpu.VMEM((3, tc, bb, d_model), jnp.float32),
                    pltpu.VMEM((2, tc, bb, ple_dim), jnp.float32),
                    pltpu.VMEM((2, tc, bb, d_model), jnp.float32),
                    pltpu.SemaphoreType.DMA((3,)),
                    pltpu.SemaphoreType.DMA((2,)),
                    pltpu.SemaphoreType.DMA((2,))]
    scratch += [pltpu.VMEM(v.shape, v.dtype) for v in late]
    scratch.append(pltpu.SemaphoreType.DMA((len(late),)))
    for dtype in (jnp.float32, jnp.bfloat16):
        scratch += [pltpu.VMEM((rows, a.shape[2]), dtype)
                    for a, (rows, _) in zip(cast_src, cast_chunks)]
    if cast_src:
        scratch.append(pltpu.SemaphoreType.DMA((2, len(cast_src))))
    kern = functools.partial(_layer_kernel, layer=layer, tc=tc, bb=bb, carry=carry, pos0=pos0,
                             final_norm=final_norm, n_ssm_blocks=n_ssm_blocks, d_ff=d_ff,
                             cast_chunks=cast_chunks)
    return pl.pallas_call(
        kern,
        grid=(n_tiles + 1 if carry else n_tiles,),
        in_specs=in_specs,
        out_specs=out_specs,
        out_shape=out_shape,
        scratch_shapes=scratch,
        compiler_params=pltpu.CompilerParams(dimension_semantics=("arbitrary",),
                                             vmem_limit_bytes=VMEM_LIMIT_BYTES),
        name=name,
    )(*args)


PROMPT_TC = 64
SAMPLE_BB = 64


def kernel(x_prompt, x_sample, state_pool, state_ssm_re, state_ssm_im, p_prompt, p_sample, g_mix, w_in, w_pool, pool_scale, ssm_a_re, ssm_a_im, ssm_log_dt, ssm_b_re, ssm_b_im, ssm_c_re, ssm_c_im, ssm_d, w_glu, b_glu, w_out, g_ffn, w_gate_up, w_down, g_ple, w_ple, w_ple_gate, g_final):
    depth = w_in.shape[0]
    batch, seq, d_model = x_prompt.shape
    dec_batch, dec_seq, _ = x_sample.shape
    n_groups, n_state = ssm_a_re.shape[1:]
    n_state_cols = n_groups * n_state
    pool_width = pool_scale.shape[-1]
    n_ssm_blocks = n_groups // SSM_BLOCK_GROUPS
    bf16 = jnp.bfloat16
    assert batch == SUBLANE and seq % PROMPT_TC == 0 and dec_batch % SAMPLE_BB == 0

    lam_re, lam_im, bb_re, bb_im = _ssm_prep(ssm_a_re, ssm_a_im, ssm_log_dt, ssm_b_re, ssm_b_im)

    def b_blocks(bb):
        w = jnp.transpose(bb, (0, 2, 1, 3)).reshape(
            depth, n_ssm_blocks, SSM_BLOCK_GROUPS * SSM_GROUP_DIM, n_state)
        return _block_diag(w, SSM_BLOCK_GROUPS)

    def c_blocks(c):
        w = jnp.transpose(c, (0, 1, 3, 2)).reshape(
            depth, n_ssm_blocks, SSM_BLOCK_GROUPS * n_state, SSM_GROUP_DIM)
        return _block_diag(w, SSM_BLOCK_GROUPS)

    b_cat = jnp.concatenate([b_blocks(bb_re), b_blocks(bb_im)], axis=-1).astype(bf16)
    c_cat = jnp.concatenate([c_blocks(ssm_c_re), -c_blocks(ssm_c_im)], axis=-2).astype(bf16)
    row = lambda a: a[:, None, :]
    n_pool, pool_dim = w_pool.shape[1:3]
    w_pool_pairs = _block_diag(
        w_pool.reshape(depth, n_pool // POOL_PAIR, POOL_PAIR * pool_dim, pool_dim), POOL_PAIR)
    g_mix_r, pool_scale_r, ssm_d_r, b_glu_r, g_ffn_r, g_ple_r = map(
        row, (g_mix, pool_scale, ssm_d, b_glu, g_ffn, g_ple))
    lam_re_r = lam_re.reshape(depth, 1, n_state_cols)
    lam_im_r = lam_im.reshape(depth, 1, n_state_cols)
    w_pool_b, w_glu_b, w_ple_b = (a.astype(bf16) for a in (w_pool_pairs, w_glu, w_ple))
    g_fin = g_final[None, :]

    cast_src = (w_in, w_out, w_gate_up, w_down, w_ple_gate)
    big = tuple(w[0].astype(bf16) for w in cast_src)

    def layer_weights(l, big):
        b_in, b_out, b_gu, b_down, b_ple_gate = big
        return (g_mix_r, b_in, w_pool_b, pool_scale_r, lam_re_r, lam_im_r, b_cat, c_cat,
                ssm_d_r, w_glu_b, b_glu_r, b_out, g_ffn_r, b_gu, b_down, g_ple_r, w_ple_b[l],
                b_ple_gate)

    xp = x_prompt
    weights_of = []
    hist_p, re_p, im_p = [], [], []
    for l in range(depth):
        weights_of.append(layer_weights(l, big))
        xp, hist, h_re, h_im, *big = _layer_call(
            l, xp, p_prompt, None, weights_of[l], g_fin, cast_src if l + 1 < depth else (),
            tc=PROMPT_TC, bb=batch, carry=True, pos0=0,
            final_norm=(l == depth - 1), name=f"prompt_layer{l}")
        hist_p.append(hist)
        re_p.append(h_re)
        im_p.append(h_im)
    pool_p = jnp.transpose(jnp.stack(hist_p).reshape(depth, POOL_BUF, batch, pool_width),
                           (0, 2, 1, 3))
    re_p = jnp.stack(re_p).reshape(depth, batch, n_groups, n_state)
    im_p = jnp.stack(im_p).reshape(depth, batch, n_groups, n_state)

    bb = SAMPLE_BB
    nt = dec_batch // bb

    def to_tiles(a):
        lead, (t, w) = a.shape[:-3], a.shape[-2:]
        a = jnp.swapaxes(a.reshape(lead + (nt, bb, t, w)), -3, -2)
        return a.reshape(lead + (nt * t * bb, w))

    def from_tiles(a, t):
        lead, w = a.shape[:-2], a.shape[-1]
        a = jnp.swapaxes(a.reshape(lead + (nt, t, bb, w)), -3, -2)
        return a.reshape(lead + (dec_batch, t, w))

    xs = to_tiles(x_sample)
    state = (to_tiles(state_pool),
             state_ssm_re.reshape(depth, dec_batch, n_state_cols),
             state_ssm_im.reshape(depth, dec_batch, n_state_cols))
    ps = to_tiles(p_sample.astype(bf16))
    hist_s, re_s, im_s = [], [], []
    for l in range(depth):
        xs, hist, h_re, h_im = _layer_call(
            l, xs, ps, state, weights_of[l], g_fin, (), tc=dec_seq, bb=bb, carry=False,
            pos0=PAST_LEN, final_norm=(l == depth - 1), name=f"sample_layer{l}")
        hist_s.append(hist)
        re_s.append(h_re)
        im_s.append(h_im)
    pool_s = from_tiles(jnp.stack(hist_s), POOL_BUF)
    re_s = jnp.stack(re_s).reshape(depth, dec_batch, n_groups, n_state)
    im_s = jnp.stack(im_s).reshape(depth, dec_batch, n_groups, n_state)

    return (xp, from_tiles(xs, dec_seq), pool_p, re_p, im_p, pool_s, re_s, im_s)
```

```python
import functools

import numpy as np
import jax
import jax.numpy as jnp
from jax import lax
from jax.experimental import pallas as pl
from jax.experimental.pallas import tpu as pltpu

LANE = 128
SUBLANE = 8
VMEM_LIMIT_BYTES = 60 * 1024 * 1024

POOL_WINDOWS = (2, 4, 8, 16)
POOL_BUF = max(POOL_WINDOWS) - 1
SSM_GROUP_DIM = 16
SSM_STATE = 64
EPS = 1e-6
PAST_LEN = 16384

SSM_BLOCK_CH = LANE
SSM_BLOCK_GROUPS = SSM_BLOCK_CH // SSM_GROUP_DIM
SSM_BLOCK_STATES = SSM_BLOCK_GROUPS * SSM_STATE

FFN_CHUNK = 256
POOL_PAIR = 2

_SQRT_HALF = np.float32(np.sqrt(0.5))


def _rms_scale(x):
    return lax.rsqrt(jnp.mean(x * x, axis=-1, keepdims=True) + EPS)


def _rms(x, g):
    return x * _rms_scale(x) * g


def _rms_dot(x, g, w):
    return _bdot(x * g, w) * _rms_scale(x)


def _bdot(a, b):
    return jnp.dot(a.astype(jnp.bfloat16), b, preferred_element_type=jnp.float32)


def _ssm_prep_kernel(a_re_ref, a_im_ref, log_dt_ref, b_re_ref, b_im_ref,
                     lam_re_ref, lam_im_ref, bb_re_ref, bb_im_ref):
    ar = a_re_ref[...]
    ai = a_im_ref[...]
    dt = jnp.exp(log_dt_ref[...])
    mag = jnp.exp(dt * ar)
    lam_re = mag * jnp.cos(dt * ai)
    lam_im = mag * jnp.sin(dt * ai)
    den = ar * ar + ai * ai
    nr = lam_re - 1.0
    k_re = (nr * ar + lam_im * ai) / den
    k_im = (lam_im * ar - nr * ai) / den
    lam_re_ref[...] = lam_re
    lam_im_ref[...] = lam_im
    for c in range(SSM_GROUP_DIM):
        br = b_re_ref[c]
        bi = b_im_ref[c]
        bb_re_ref[c] = k_re * br - k_im * bi
        bb_im_ref[c] = k_re * bi + k_im * br


def _ssm_prep(a_re, a_im, log_dt, b_re, b_im):
    depth, n_groups, n_state = a_re.shape
    b_re_t = jnp.transpose(b_re, (0, 3, 1, 2))
    b_im_t = jnp.transpose(b_im, (0, 3, 1, 2))
    gp = pl.BlockSpec((None, n_groups, n_state), lambda l: (l, 0, 0))
    cgp = pl.BlockSpec((None, SSM_GROUP_DIM, n_groups, n_state), lambda l: (l, 0, 0, 0))
    return pl.pallas_call(
        _ssm_prep_kernel,
        grid=(depth,),
        in_specs=[gp, gp, pl.BlockSpec((None, n_groups, 1), lambda l: (l, 0, 0)), cgp, cgp],
        out_specs=[gp, gp, cgp, cgp],
        out_shape=[jax.ShapeDtypeStruct(a_re.shape, jnp.float32)] * 2
        + [jax.ShapeDtypeStruct(b_re_t.shape, jnp.float32)] * 2,
        name="ssm_prep",
    )(a_re, a_im, log_dt[..., None], b_re_t, b_im_t)


def _block_diag(w, gb):
    rows, c = w.shape[-2:]
    r = rows // gb
    same_block = (jnp.arange(rows)[:, None] // r) == (jnp.arange(gb * c)[None, :] // c)
    return jnp.where(same_block, jnp.tile(w, (1, 1, 1, gb)), 0.0)


def _start_all(copies):
    for c in copies:
        c.start()


def _wait_all(copies):
    for c in copies:
        c.wait()


def _layer_kernel(*refs, layer, tc, bb, carry, pos0, final_norm, n_ssm_blocks, d_ff,
                  cast_chunks):
    it = iter(refs)
    x_ref = next(it)
    p_ref = next(it)
    if not carry:
        hist_in_ref = next(it)
        h_re_in_ref = next(it)
        h_im_in_ref = next(it)
    (g_mix_ref, w_in_ref, w_pool_ref, pool_scale_ref, lam_re_ref, lam_im_ref,
     b_ref, c_ref, d_skip_ref, w_glu_ref, b_glu_ref, w_out_ref, g_ffn_ref,
     w_gu_ref, w_down_ref, g_ple_ref, w_ple_ref, w_ple_gate_ref, g_final_ref) = (
         next(it) for _ in range(19))
    cast_src = tuple(next(it) for _ in cast_chunks)
    o_ref = next(it)
    hist_out_ref = next(it)
    h_re_ref = next(it)
    h_im_ref = next(it)
    cast_dst = tuple(next(it) for _ in cast_chunks)
    ue_ref = next(it)
    bu_ref = next(it)
    mix_ref = next(it)
    act_ref = next(it)
    lam_b_ref = next(it)
    if carry:
        xbuf_ref = next(it)
        pbuf_ref = next(it)
        obuf_ref = next(it)
        x_sem = next(it)
        p_sem = next(it)
        o_sem = next(it)
    late_hbm = (w_out_ref, w_gu_ref, w_down_ref, w_ple_ref, w_ple_gate_ref)
    late_buf = tuple(next(it) for _ in late_hbm)
    w_sem = next(it)
    w_out_ref, w_gu_ref, w_down_ref, w_ple_ref, w_ple_gate_ref = late_buf

    def late_copies():
        return [pltpu.make_async_copy(h, v, w_sem.at[i])
                for i, (h, v) in enumerate(zip(late_hbm, late_buf))]

    cast_f32 = tuple(next(it) for _ in cast_chunks)
    cast_b16 = tuple(next(it) for _ in cast_chunks)
    if cast_chunks:
        cast_sem = next(it)

    m = tc * bb
    hb = POOL_BUF * bb
    pool_width = ue_ref.shape[1]
    bs = SSM_BLOCK_STATES
    n_cols = n_ssm_blocks * bs
    pid = pl.program_id(0)

    def stage_a(x, a_slot, tile):
        z = _rms_dot(x, g_mix_ref[...], w_in_ref[...])
        ue_ref[hb:hb + m, :] = z[:, :pool_width]
        u_ssm = z[:, pool_width:]

        t_local = lax.shift_right_logical(
            lax.broadcasted_iota(jnp.int32, (m, LANE), 0), int(np.log2(bb)))
        pos = t_local + (pos0 + tile * tc)
        diffs = []
        for g, w in enumerate(POOL_WINDOWS):
            cols = slice(g * LANE, (g + 1) * LANE)
            cur = ue_ref[hb:hb + m, cols]
            s = cur
            for j in range(1, w):
                s = s + ue_ref[hb - j * bb:hb - j * bb + m, cols]
            cnt = jnp.minimum(pos + 1, w).astype(jnp.float32)
            diffs.append((s / cnt - cur).astype(jnp.bfloat16))
        for q in range(len(POOL_WINDOWS) // POOL_PAIR):
            cols = slice(q * POOL_PAIR * LANE, (q + 1) * POOL_PAIR * LANE)
            d = jnp.concatenate(diffs[q * POOL_PAIR:(q + 1) * POOL_PAIR], axis=-1)
            y = jnp.dot(d, w_pool_ref[q], preferred_element_type=jnp.float32)
            mix_ref[a_slot, :, cols] = (y * pool_scale_ref[:, cols]).astype(jnp.bfloat16)
        hist_out_ref[...] = ue_ref[m:m + hb, :]
        if carry:
            ue_ref[0:hb, :] = ue_ref[m:m + hb, :]

        for i in range(n_ssm_blocks):
            u_i = u_ssm[:, i * SSM_BLOCK_CH:(i + 1) * SSM_BLOCK_CH]
            bu_ref[:, 2 * bs * i:2 * bs * (i + 1)] = _bdot(u_i, b_ref[i])
        lam_b_ref[0] = jnp.broadcast_to(lam_re_ref[...], (SUBLANE, n_cols))
        lam_b_ref[1] = jnp.broadcast_to(lam_im_ref[...], (SUBLANE, n_cols))
        for rg in range(bb // SUBLANE):
            r_off = rg * SUBLANE

            def step(t, h, r_off=r_off):
                rows = pl.ds(pl.multiple_of(t * bb + r_off, SUBLANE), SUBLANE)
                new = []
                for i in range(n_ssm_blocks):
                    c_re, c_im, s0 = 2 * bs * i, 2 * bs * i + bs, bs * i
                    h_re, h_im = h[2 * i], h[2 * i + 1]
                    lre = lam_b_ref[0, :, s0:s0 + bs]
                    lim = lam_b_ref[1, :, s0:s0 + bs]
                    n_re = lre * h_re - lim * h_im + bu_ref[rows, c_re:c_re + bs]
                    n_im = lre * h_im + lim * h_re + bu_ref[rows, c_im:c_im + bs]
                    bu_ref[rows, c_re:c_re + bs] = n_re
                    bu_ref[rows, c_im:c_im + bs] = n_im
                    new += [n_re, n_im]
                return tuple(new)

            h0 = []
            for i in range(n_ssm_blocks):
                h0 += [h_re_ref[r_off:r_off + SUBLANE, bs * i:bs * (i + 1)],
                       h_im_ref[r_off:r_off + SUBLANE, bs * i:bs * (i + 1)]]
            h = lax.fori_loop(0, tc, step, tuple(h0), unroll=True if carry else min(tc, 8))
            for i in range(n_ssm_blocks):
                h_re_ref[r_off:r_off + SUBLANE, bs * i:bs * (i + 1)] = h[2 * i]
                h_im_ref[r_off:r_off + SUBLANE, bs * i:bs * (i + 1)] = h[2 * i + 1]
        ys = []
        for i in range(n_ssm_blocks):
            cols = slice(i * SSM_BLOCK_CH, (i + 1) * SSM_BLOCK_CH)
            ys.append(_bdot(bu_ref[:, 2 * bs * i:2 * bs * (i + 1)], c_ref[i])
                      + d_skip_ref[:, cols] * u_ssm[:, cols])
        y = jnp.concatenate(ys, axis=-1)
        gl = 0.5 * y * (1.0 + lax.erf(y * _SQRT_HALF))
        y_ssm = gl * jax.nn.sigmoid(_bdot(gl, w_glu_ref[...]) + b_glu_ref[...])
        mix_ref[a_slot, :, pool_width:] = y_ssm.astype(jnp.bfloat16)

    def stage_b(x, p, b_slot):
        x = x + jnp.dot(mix_ref[b_slot], w_out_ref[...], preferred_element_type=jnp.float32)
        hn = (x * g_ffn_ref[...]).astype(jnp.bfloat16)
        hs = _rms_scale(x)
        for j in range(d_ff // FFN_CHUNK):
            c0 = j * FFN_CHUNK
            gate = hs * jnp.dot(hn, w_gu_ref[:, c0:c0 + FFN_CHUNK],
                                preferred_element_type=jnp.float32)
            up = hs * jnp.dot(hn, w_gu_ref[:, d_ff + c0:d_ff + c0 + FFN_CHUNK],
                              preferred_element_type=jnp.float32)
            act_ref[:, c0:c0 + FFN_CHUNK] = (
                gate * jax.nn.sigmoid(gate) * up).astype(jnp.bfloat16)
        x = x + jnp.dot(act_ref[...], w_down_ref[...], preferred_element_type=jnp.float32)
        gate = jax.nn.sigmoid(_rms_dot(x, g_ple_ref[...], w_ple_gate_ref[...]))
        x = x + _bdot(p, w_ple_ref[...]) * gate
        if final_norm:
            x = _rms(x, g_final_ref[...])
        return x

    if not carry:
        @pl.when(pid == 0)
        def _():
            _start_all(late_copies())

        ue_ref[0:hb, :] = hist_in_ref[...]
        h_re_ref[...] = h_re_in_ref[...]
        h_im_ref[...] = h_im_in_ref[...]
        x = x_ref[...]
        stage_a(x, 0, 0)

        @pl.when(pid == 0)
        def _():
            _wait_all(late_copies())

        o_ref[...] = stage_b(x, p_ref[...], 0)
        return

    n = pl.num_programs(0) - 1

    def x_copies(tile):
        s = lax.rem(tile, 3)
        return [pltpu.make_async_copy(x_ref.at[b, pl.ds(tile * tc, tc), :],
                                      xbuf_ref.at[s, :, b, :], x_sem.at[s]) for b in range(bb)]

    def p_copies(tile):
        s = lax.rem(tile, 2)
        return [pltpu.make_async_copy(p_ref.at[layer, b, pl.ds(tile * tc, tc), :],
                                      pbuf_ref.at[s, :, b, :], p_sem.at[s]) for b in range(bb)]

    def o_copies(tile):
        s = lax.rem(tile, 2)
        return [pltpu.make_async_copy(obuf_ref.at[s, :, b, :],
                                      o_ref.at[b, pl.ds(tile * tc, tc), :], o_sem.at[s])
                for b in range(bb)]

    def x_tile(tile):
        return xbuf_ref[lax.rem(tile, 3)].reshape(m, xbuf_ref.shape[-1])

    def run_b(tile):
        _wait_all(p_copies(tile))
        out = stage_b(x_tile(tile), pbuf_ref[lax.rem(tile, 2)].reshape(m, pbuf_ref.shape[-1]),
                      lax.rem(tile, 2))
        obuf_ref[lax.rem(tile, 2)] = out.reshape(tc, bb, out.shape[-1])
        _start_all(o_copies(tile))

    @pl.when(pid == 0)
    def _():
        _start_all(x_copies(0) + late_copies())
        ue_ref[0:hb, :] = jnp.zeros((hb, pool_width), jnp.float32)
        h_re_ref[...] = jnp.zeros(h_re_ref.shape, jnp.float32)
        h_im_ref[...] = jnp.zeros(h_im_ref.shape, jnp.float32)

    @pl.when(pid + 1 < n)
    def _():
        _start_all(x_copies(pid + 1))

    @pl.when(pid < n)
    def _():
        _start_all(p_copies(pid))

    @pl.when(pid >= 3)
    def _():
        _wait_all(o_copies(pid - 3))

    @pl.when(pid == 1)
    def _():
        _wait_all(late_copies())

    def cast_rows(i, chunk):
        rows = cast_chunks[i]
        last = cast_dst[i].shape[0] - rows
        return pl.ds(pl.multiple_of(jnp.minimum(chunk * rows, last), 2 * SUBLANE), rows)

    def cast_in(chunk):
        return [pltpu.make_async_copy(cast_src[i].at[layer + 1, cast_rows(i, chunk), :],
                                      cast_f32[i], cast_sem.at[0, i])
                for i in range(len(cast_chunks))]

    def cast_out(chunk):
        return [pltpu.make_async_copy(cast_b16[i], cast_dst[i].at[cast_rows(i, chunk), :],
                                      cast_sem.at[1, i])
                for i in range(len(cast_chunks))]

    def cast_top(step):
        _wait_all(cast_in(step))
        for f32_ref, b16_ref in zip(cast_f32, cast_b16):
            b16_ref[...] = f32_ref[...].astype(jnp.bfloat16)
        _start_all(cast_out(step) + cast_in(step + 1))

    @pl.when(pid == 0)
    def _():
        _start_all(cast_in(0))
        _wait_all(x_copies(0))
        cast_top(0)
        stage_a(x_tile(0), 0, 0)
        _wait_all(cast_out(0))

    @pl.when((pid > 0) & (pid < n))
    def _():
        _wait_all(x_copies(pid))
        cast_top(pid)
        stage_a(x_tile(pid), lax.rem(pid, 2), pid)
        run_b(pid - 1)
        _wait_all(cast_out(pid))

    @pl.when(pid == n)
    def _():
        _wait_all(cast_in(n))
        run_b(n - 1)

        @pl.when(n >= 2)
        def _():
            _wait_all(o_copies(n - 2))

        _wait_all(o_copies(n - 1))


def _cast_chunk_rows(rows, max_chunks):
    bf16_tile_rows = 2 * SUBLANE
    assert rows % bf16_tile_rows == 0
    for r in range(bf16_tile_rows, rows + 1, bf16_tile_rows):
        if pl.cdiv(rows, r) <= max_chunks:
            return r
    raise ValueError(f"no chunking of {rows} rows into <= {max_chunks} chunks")


def _layer_call(layer, x, p, state, weights, g_final, cast_src, *, tc, bb, carry, pos0,
                final_norm, name):
    m = tc * bb
    hb = POOL_BUF * bb
    d_model = x.shape[-1]
    ple_dim = p.shape[-1]
    n_tiles = (x.shape[1] // tc) if carry else (x.shape[0] // m)
    (g_mix, w_in, w_pool, pool_scale, lam_re, lam_im, b_cat, c_cat, d_skip, w_glu, b_glu,
     w_out, g_ffn, w_gu, w_down, g_ple, w_ple, w_ple_gate) = weights
    pool_width = pool_scale.shape[-1]
    n_ssm_blocks = b_cat.shape[1]
    n_state_cols = lam_re.shape[-1]
    d_ff = w_down.shape[0]
    late = (w_out, w_gu, w_down, w_ple, w_ple_gate)

    hbm = pl.BlockSpec(memory_space=pl.ANY)

    def weight_spec(arr):
        if any(arr is v for v in late):
            return hbm
        if arr is w_in:
            return pl.BlockSpec(arr.shape, lambda i: (0, 0), pipeline_mode=pl.Buffered(1))
        nd = arr.ndim - 1
        return pl.BlockSpec((None,) + arr.shape[1:], lambda i, nd=nd: (layer,) + (0,) * nd,
                            pipeline_mode=pl.Buffered(1))

    if carry:
        in_specs = [hbm, hbm]
        args = [x, p]
        state_spec = lambda r, c: pl.BlockSpec((r, c), lambda i: (0, 0))
        x_out_spec = hbm
    else:
        in_specs = [pl.BlockSpec((m, d_model), lambda i: (i, 0)),
                    pl.BlockSpec((None, m, ple_dim), lambda i: (layer, i, 0))]
        layer_tile = lambda r, c: pl.BlockSpec((None, r, c), lambda i: (layer, i, 0))
        in_specs += [layer_tile(hb, pool_width), layer_tile(bb, n_state_cols),
                     layer_tile(bb, n_state_cols)]
        args = [x, p] + list(state)
        state_spec = lambda r, c: pl.BlockSpec((r, c), lambda i: (i, 0))
        x_out_spec = pl.BlockSpec((m, d_model), lambda i: (i, 0))
    in_specs += [weight_spec(w) for w in weights]
    in_specs.append(pl.BlockSpec(g_final.shape, lambda i: (0, 0), pipeline_mode=pl.Buffered(1)))
    in_specs += [hbm] * len(cast_src)
    args += list(weights) + [g_final] + list(cast_src)

    cast_chunks = tuple(_cast_chunk_rows(a.shape[1], n_tiles) for a in cast_src)
    n_state_rows = bb if carry else n_tiles * bb
    n_hist_rows = hb if carry else n_tiles * hb
    out_shape = [jax.ShapeDtypeStruct(x.shape, jnp.float32),
                 jax.ShapeDtypeStruct((n_hist_rows, pool_width), jnp.float32),
                 jax.ShapeDtypeStruct((n_state_rows, n_state_cols), jnp.float32),
                 jax.ShapeDtypeStruct((n_state_rows, n_state_cols), jnp.float32)]
    out_shape += [jax.ShapeDtypeStruct(a.shape[1:], jnp.bfloat16) for a in cast_src]
    out_specs = [x_out_spec, state_spec(hb, pool_width), state_spec(bb, n_state_cols),
                 state_spec(bb, n_state_cols)] + [hbm] * len(cast_src)
    scratch = [pltpu.VMEM((hb + m, pool_width), jnp.float32),
               pltpu.VMEM((m, 2 * n_state_cols), jnp.float32),
               pltpu.VMEM((2 if carry else 1, m, w_out.shape[0]), jnp.bfloat16),
               pltpu.VMEM((m, d_ff), jnp.bfloat16),
               pltpu.VMEM((2, SUBLANE, n_state_cols), jnp.float32)]
    if carry:
        scratch += [pltpu.VMEM((3, tc, bb, d_model), jnp.float32),
                    pltpu.VMEM((2, tc, bb, ple_dim), jnp.float32),
                    pltpu.VMEM((2, tc, bb, d_model), jnp.float32),
                    pltpu.SemaphoreType.DMA((3,)),
                    pltpu.SemaphoreType.DMA((2,)),
                    pltpu.SemaphoreType.DMA((2,))]
    scratch += [pltpu.VMEM(v.shape, v.dtype) for v in late]
    scratch.append(pltpu.SemaphoreType.DMA((len(late),)))
    for dtype in (jnp.float32, jnp.bfloat16):
        scratch += [pltpu.VMEM((rows, a.shape[2]), dtype)
                    for a, rows in zip(cast_src, cast_chunks)]
    if cast_src:
        scratch.append(pltpu.SemaphoreType.DMA((2, len(cast_src))))
    kern = functools.partial(_layer_kernel, layer=layer, tc=tc, bb=bb, carry=carry, pos0=pos0,
                             final_norm=final_norm, n_ssm_blocks=n_ssm_blocks, d_ff=d_ff,
                             cast_chunks=cast_chunks)
    return pl.pallas_call(
        kern,
        grid=(n_tiles + 1 if carry else n_tiles,),
        in_specs=in_specs,
        out_specs=out_specs,
        out_shape=out_shape,
        scratch_shapes=scratch,
        compiler_params=pltpu.CompilerParams(dimension_semantics=("arbitrary",),
                                             vmem_limit_bytes=VMEM_LIMIT_BYTES),
        name=name,
    )(*args)


PROMPT_TC = 64
SAMPLE_BB = 64


def kernel(x_prompt, x_sample, state_pool, state_ssm_re, state_ssm_im, p_prompt, p_sample, g_mix, w_in, w_pool, pool_scale, ssm_a_re, ssm_a_im, ssm_log_dt, ssm_b_re, ssm_b_im, ssm_c_re, ssm_c_im, ssm_d, w_glu, b_glu, w_out, g_ffn, w_gate_up, w_down, g_ple, w_ple, w_ple_gate, g_final):
    depth = w_in.shape[0]
    batch, seq, d_model = x_prompt.shape
    dec_batch, dec_seq, _ = x_sample.shape
    n_groups, n_state = ssm_a_re.shape[1:]
    n_state_cols = n_groups * n_state
    pool_width = pool_scale.shape[-1]
    n_ssm_blocks = n_groups // SSM_BLOCK_GROUPS
    bf16 = jnp.bfloat16
    assert batch == SUBLANE and seq % PROMPT_TC == 0 and dec_batch % SAMPLE_BB == 0

    lam_re, lam_im, bb_re, bb_im = _ssm_prep(ssm_a_re, ssm_a_im, ssm_log_dt, ssm_b_re, ssm_b_im)

    def b_blocks(bb):
        w = jnp.transpose(bb, (0, 2, 1, 3)).reshape(
            depth, n_ssm_blocks, SSM_BLOCK_GROUPS * SSM_GROUP_DIM, n_state)
        return _block_diag(w, SSM_BLOCK_GROUPS)

    def c_blocks(c):
        w = jnp.transpose(c, (0, 1, 3, 2)).reshape(
            depth, n_ssm_blocks, SSM_BLOCK_GROUPS * n_state, SSM_GROUP_DIM)
        return _block_diag(w, SSM_BLOCK_GROUPS)

    b_cat = jnp.concatenate([b_blocks(bb_re), b_blocks(bb_im)], axis=-1).astype(bf16)
    c_cat = jnp.concatenate([c_blocks(ssm_c_re), -c_blocks(ssm_c_im)], axis=-2).astype(bf16)
    row = lambda a: a[:, None, :]
    n_pool, pool_dim = w_pool.shape[1:3]
    w_pool_pairs = _block_diag(
        w_pool.reshape(depth, n_pool // POOL_PAIR, POOL_PAIR * pool_dim, pool_dim), POOL_PAIR)
    g_mix_r, pool_scale_r, ssm_d_r, b_glu_r, g_ffn_r, g_ple_r = map(
        row, (g_mix, pool_scale, ssm_d, b_glu, g_ffn, g_ple))
    lam_re_r = lam_re.reshape(depth, 1, n_state_cols)
    lam_im_r = lam_im.reshape(depth, 1, n_state_cols)
    w_pool_b, w_glu_b, w_ple_b = (a.astype(bf16) for a in (w_pool_pairs, w_glu, w_ple))
    g_fin = g_final[None, :]

    cast_src = (w_in, w_out, w_gate_up, w_down, w_ple_gate)
    big = tuple(w[0].astype(bf16) for w in cast_src)

    def layer_weights(l, big):
        b_in, b_out, b_gu, b_down, b_ple_gate = big
        return (g_mix_r, b_in, w_pool_b, pool_scale_r, lam_re_r, lam_im_r, b_cat, c_cat,
                ssm_d_r, w_glu_b, b_glu_r, b_out, g_ffn_r, b_gu, b_down, g_ple_r, w_ple_b[l],
                b_ple_gate)

    xp = x_prompt
    weights_of = []
    hist_p, re_p, im_p = [], [], []
    for l in range(depth):
        weights_of.append(layer_weights(l, big))
        xp, hist, h_re, h_im, *big = _layer_call(
            l, xp, p_prompt, None, weights_of[l], g_fin, cast_src if l + 1 < depth else (),
            tc=PROMPT_TC, bb=batch, carry=True, pos0=0,
            final_norm=(l == depth - 1), name=f"prompt_layer{l}")
        hist_p.append(hist)
        re_p.append(h_re)
        im_p.append(h_im)
    pool_p = jnp.transpose(jnp.stack(hist_p).reshape(depth, POOL_BUF, batch, pool_width),
                           (0, 2, 1, 3))
    re_p = jnp.stack(re_p).reshape(depth, batch, n_groups, n_state)
    im_p = jnp.stack(im_p).reshape(depth, batch, n_groups, n_state)

    bb = SAMPLE_BB
    nt = dec_batch // bb

    def to_tiles(a):
        lead, (t, w) = a.shape[:-3], a.shape[-2:]
        a = jnp.swapaxes(a.reshape(lead + (nt, bb, t, w)), -3, -2)
        return a.reshape(lead + (nt * t * bb, w))

    def from_tiles(a, t):
        lead, w = a.shape[:-2], a.shape[-1]
        a = jnp.swapaxes(a.reshape(lead + (nt, t, bb, w)), -3, -2)
        return a.reshape(lead + (dec_batch, t, w))

    xs = to_tiles(x_sample)
    state = (to_tiles(state_pool),
             state_ssm_re.reshape(depth, dec_batch, n_state_cols),
             state_ssm_im.reshape(depth, dec_batch, n_state_cols))
    ps = to_tiles(p_sample.astype(bf16))
    hist_s, re_s, im_s = [], [], []
    for l in range(depth):
        xs, hist, h_re, h_im = _layer_call(
            l, xs, ps, state, weights_of[l], g_fin, (), tc=dec_seq, bb=bb, carry=False,
            pos0=PAST_LEN, final_norm=(l == depth - 1), name=f"sample_layer{l}")
        hist_s.append(hist)
        re_s.append(h_re)
        im_s.append(h_im)
    pool_s = from_tiles(jnp.stack(hist_s), POOL_BUF)
    re_s = jnp.stack(re_s).reshape(depth, dec_batch, n_groups, n_state)
    im_s = jnp.stack(im_s).reshape(depth, dec_batch, n_groups, n_state)

    return (xp, from_tiles(xs, dec_seq), pool_p, re_p, im_p, pool_s, re_s, im_s)
```
